```python
import math
import jax, jax.numpy as jnp
from jax import lax
import numpy as np

D_MODEL = 4096
BATCH = 4
SEQ = 2048
DEPTH = 2
DEC_BATCH = 8
DEC_SEQ = 8
PAST_LEN = 16384
PAGE_SIZE = 128

N_MIXERS = 2
N_SB_LAYERS = (DEPTH + N_MIXERS - 1) // N_MIXERS
N_ML_LAYERS = DEPTH // N_MIXERS
D_FF = 11008
SB_HEADS = 32
SB_HEAD_DIM = D_MODEL // SB_HEADS
SB_QBLOCK = 128
SB_BIAS_INIT = -6.0
ML_HEADS = 8
ML_DV = D_MODEL // ML_HEADS
ML_DK = ML_DV // 2
ML_CHUNK = 128
ML_IN_WIDTH = 2 * ML_HEADS * ML_DK + ML_HEADS * ML_DV + D_MODEL + 2 * ML_HEADS
NORM_EPS = 1e-6
FORGET_BIAS = 3.0

kernel_name = 'stickbreak_mlstm_macaron_step'


def rmsnorm(x, g):
    x32 = x.astype(jnp.float32)
    y = x32 * lax.rsqrt(jnp.mean(x32 * x32, axis=-1, keepdims=True) + NORM_EPS)
    return (y * g.astype(jnp.float32)).astype(x.dtype)


def macaron_half(x, g, wg, wu, wd):
    h = rmsnorm(x, g)
    return x + 0.5 * ((jax.nn.silu(h @ wg) * (h @ wu)) @ wd)


def sb_project(xn, w_in, q_g, k_g):
    B, T, _ = xn.shape
    qkv = (xn @ w_in).reshape(B, T, 3, SB_HEADS, SB_HEAD_DIM)
    return rmsnorm(qkv[:, :, 0], q_g), rmsnorm(qkv[:, :, 1], k_g), qkv[:, :, 2]


def stick_breaking_attend(q, k_parts, v_parts, q_offset, bias):
    B, T, H, Dh = q.shape
    sizes = [kk.shape[1] for kk in k_parts]
    total = sum(sizes)
    qb = SB_QBLOCK if T % SB_QBLOCK == 0 else T
    nb = T // qb
    q_blocks = jnp.moveaxis(q.reshape(B, nb, qb, H, Dh), 1, 0)
    pos_blocks = (q_offset + jnp.arange(T, dtype=jnp.int32)).reshape(nb, qb)
    key_pos = jnp.arange(total, dtype=jnp.int32)
    scale = 1.0 / math.sqrt(Dh)
    b_h = bias.astype(jnp.float32)[None, :, None, None]

    def one_block(args):
        q_blk, pos = args
        z = jnp.concatenate([jnp.einsum('bqhd,bshd->bhqs', q_blk, kk) for kk in k_parts], axis=-1).astype(jnp.float32) * scale + b_h
        mask = key_pos[None, :] < pos[:, None]
        log_keep = jnp.where(mask, jax.nn.log_sigmoid(-z), 0.0)
        log_after = lax.cumsum(log_keep, axis=3, reverse=True) - log_keep
        w = jnp.where(mask, jnp.exp(jax.nn.log_sigmoid(z) + log_after), 0.0)
        outs = []
        start = 0
        for size, vv in zip(sizes, v_parts):
            outs.append(jnp.einsum('bhqs,bshd->bqhd', w[..., start:start + size].astype(vv.dtype), vv))
            start += size
        return sum(outs[1:], outs[0])

    out = lax.map(one_block, (q_blocks, pos_blocks))
    return jnp.moveaxis(out, 0, 1).reshape(B, T, H * Dh)


def ml_project(xn, w_in, b_gates):
    B, T, _ = xn.shape
    p = xn @ w_in
    s_qk = ML_HEADS * ML_DK
    s_v = ML_HEADS * ML_DV
    q = p[..., :s_qk].reshape(B, T, ML_HEADS, ML_DK)
    k = p[..., s_qk:2 * s_qk].reshape(B, T, ML_HEADS, ML_DK) * (ML_DK ** -0.5)
    v = p[..., 2 * s_qk:2 * s_qk + s_v].reshape(B, T, ML_HEADS, ML_DV)
    o = p[..., 2 * s_qk + s_v:2 * s_qk + s_v + D_MODEL]
    gates = p[..., 2 * s_qk + s_v + D_MODEL:].astype(jnp.float32).reshape(B, T, 2, ML_HEADS) + b_gates.astype(jnp.float32)
    return q, k, v, o, gates[:, :, 0], gates[:, :, 1]


def mlstm_chunkwise(q, k, v, i_pre, f_pre, C0, n0, m0):
    B, T, H, DK = q.shape
    DV = v.shape[-1]
    L = ML_CHUNK if T % ML_CHUNK == 0 else T
    nc = T // L

    def to_chunks(a):
        a = a.astype(jnp.float32).reshape((B, nc, L) + a.shape[2:])
        return jnp.swapaxes(jnp.moveaxis(a, 1, 0), 2, 3)

    causal = jnp.tril(jnp.ones((L, L), dtype=bool))

    def step(carry, xs):
        C, n, m = carry
        qc, kc, vc, ic, fc = xs
        b = jnp.cumsum(jax.nn.log_sigmoid(fc), axis=-1)
        D = jnp.where(causal, b[..., :, None] - b[..., None, :] + ic[..., None, :], -jnp.inf)
        g = b + m[..., None]
        m_t = jnp.maximum(g, jnp.max(D, axis=-1))
        W = jnp.exp(D - m_t[..., None]) * jnp.einsum('bhtk,bhsk->bhts', qc, kc)
        w_carry = jnp.exp(g - m_t)
        num = w_carry[..., None] * jnp.einsum('bhvk,bhtk->bhtv', C, qc) + jnp.einsum('bhts,bhsv->bhtv', W, vc)
        den = w_carry * jnp.einsum('bhk,bhtk->bht', n, qc) + jnp.sum(W, axis=-1)
        h = num / jnp.maximum(jnp.abs(den), jnp.exp(-m_t))[..., None]
        m_new = m_t[..., -1]
        decay = jnp.exp(b[..., -1:] - b + ic - m_new[..., None])
        carry_scale = jnp.exp(b[..., -1] + m - m_new)
        C_new = carry_scale[..., None, None] * C + jnp.einsum('bhs,bhsv,bhsk->bhvk', decay, vc, kc)
        n_new = carry_scale[..., None] * n + jnp.einsum('bhs,bhsk->bhk', decay, kc)
        return (C_new, n_new, m_new), h

    init = (C0.astype(jnp.float32), n0.astype(jnp.float32), m0.astype(jnp.float32))
    (C, n, m), h = lax.scan(step, init, (to_chunks(q), to_chunks(k), to_chunks(v), to_chunks(i_pre), to_chunks(f_pre)))
    h = jnp.swapaxes(jnp.moveaxis(h, 0, 1), 2, 3).reshape(B, T, H, DV)
    return h, C, n, m


def mlstm_mixer(xn, w_in, b_gates, h_g, w_out, C0, n0, m0):
    B, T, _ = xn.shape
    q, k, v, o, i_pre, f_pre = ml_project(xn, w_in, b_gates)
    h, C, n, m = mlstm_chunkwise(q, k, v, i_pre, f_pre, C0, n0, m0)
    h = rmsnorm(h, h_g.reshape(ML_HEADS, ML_DV)).reshape(B, T, D_MODEL).astype(xn.dtype)
    return (jax.nn.sigmoid(o) * h) @ w_out, C, n, m


def setup_inputs(seed: int = 0) -> dict:
    key = jax.random.key(seed)
    ks = jax.random.split(key, 24)
    n_pages = PAST_LEN // PAGE_SIZE
    n_phys = (DEC_BATCH * n_pages * 5) // 4
    nrm = jax.random.normal
    f32 = jnp.float32
    page_table = jax.random.permutation(ks[7], n_phys)[:DEC_BATCH * n_pages].reshape(DEC_BATCH, n_pages).astype(jnp.int32)
    b_gates = jnp.stack([0.1 * nrm(ks[17], (N_ML_LAYERS, ML_HEADS), f32),
                         FORGET_BIAS + 0.5 * nrm(ks[18], (N_ML_LAYERS, ML_HEADS), f32)], axis=1)
    return {
        'x_prompt': nrm(ks[0], (BATCH, SEQ, D_MODEL), f32),
        'x_sample': nrm(ks[1], (DEC_BATCH, DEC_SEQ, D_MODEL), f32),
        'cache_k': nrm(ks[2], (N_SB_LAYERS, n_phys, PAGE_SIZE, SB_HEADS, SB_HEAD_DIM), f32),
        'cache_v': nrm(ks[3], (N_SB_LAYERS, n_phys, PAGE_SIZE, SB_HEADS, SB_HEAD_DIM), f32),
        'state_C': 0.1 * nrm(ks[4], (N_ML_LAYERS, DEC_BATCH, ML_HEADS, ML_DV, ML_DK), f32),
        'state_n': 0.5 * nrm(ks[5], (N_ML_LAYERS, DEC_BATCH, ML_HEADS, ML_DK), f32),
        'state_m': nrm(ks[6], (N_ML_LAYERS, DEC_BATCH, ML_HEADS), f32),
        'page_table': page_table,
        'norm_g': 1.0 + 0.02 * nrm(ks[8], (DEPTH, 3, D_MODEL), f32),
        'w_ffn_gate': nrm(ks[9], (DEPTH, 2, D_MODEL, D_FF), f32) * D_MODEL ** -0.5,
        'w_ffn_up': nrm(ks[10], (DEPTH, 2, D_MODEL, D_FF), f32) * D_MODEL ** -0.5,
        'w_ffn_down': nrm(ks[11], (DEPTH, 2, D_FF, D_MODEL), f32) * D_FF ** -0.5,
        'sb_w_in': nrm(ks[12], (N_SB_LAYERS, D_MODEL, 3 * D_MODEL), f32) * D_MODEL ** -0.5,
        'sb_q_norm': 1.0 + 0.02 * nrm(ks[13], (N_SB_LAYERS, SB_HEAD_DIM), f32),
        'sb_k_norm': 1.0 + 0.02 * nrm(ks[14], (N_SB_LAYERS, SB_HEAD_DIM), f32),
        'sb_logit_bias': SB_BIAS_INIT + 0.3 * nrm(ks[21], (N_SB_LAYERS, SB_HEADS), f32),
        'sb_w_out': nrm(ks[15], (N_SB_LAYERS, D_MODEL, D_MODEL), f32) * D_MODEL ** -0.5,
        'ml_w_in': nrm(ks[16], (N_ML_LAYERS, D_MODEL, ML_IN_WIDTH), f32) * D_MODEL ** -0.5,
        'ml_b_gates': b_gates,
        'ml_h_norm': 1.0 + 0.02 * nrm(ks[19], (N_ML_LAYERS, D_MODEL), f32),
        'ml_w_out': nrm(ks[20], (N_ML_LAYERS, D_MODEL, D_MODEL), f32) * D_MODEL ** -0.5,
    }


def reference(x_prompt, x_sample, cache_k, cache_v, state_C, state_n, state_m, page_table,
              norm_g, w_ffn_gate, w_ffn_up, w_ffn_down,
              sb_w_in, sb_q_norm, sb_k_norm, sb_logit_bias, sb_w_out,
              ml_w_in, ml_b_gates, ml_h_norm, ml_w_out):
    xp, xs = x_prompt, x_sample
    bp = x_prompt.shape[0]
    bs = x_sample.shape[0]
    past_len = page_table.shape[1] * cache_k.shape[2]
    kp_rows, vp_rows, ks_rows, vs_rows = [], [], [], []
    Cp, np_, mp, Cs, ns, ms = [], [], [], [], [], []
    for l in range(DEPTH):
        xp = macaron_half(xp, norm_g[l, 0], w_ffn_gate[l, 0], w_ffn_up[l, 0], w_ffn_down[l, 0])
        xs = macaron_half(xs, norm_g[l, 0], w_ffn_gate[l, 0], w_ffn_up[l, 0], w_ffn_down[l, 0])
        if l % N_MIXERS == 0:
            a = l // N_MIXERS
            q, k, v = sb_project(rmsnorm(xp, norm_g[l, 1]), sb_w_in[a], sb_q_norm[a], sb_k_norm[a])
            xp = xp + stick_breaking_attend(q, (k,), (v,), 0, sb_logit_bias[a]) @ sb_w_out[a]
            kp_rows.append(k)
            vp_rows.append(v)
            q, k, v = sb_project(rmsnorm(xs, norm_g[l, 1]), sb_w_in[a], sb_q_norm[a], sb_k_norm[a])
            k_past = cache_k[a][page_table].reshape(bs, past_len, SB_HEADS, SB_HEAD_DIM)
            v_past = cache_v[a][page_table].reshape(bs, past_len, SB_HEADS, SB_HEAD_DIM)
            xs = xs + stick_breaking_attend(q, (k_past, k), (v_past, v), past_len, sb_logit_bias[a]) @ sb_w_out[a]
            ks_rows.append(k)
            vs_rows.append(v)
        else:
            j = l // N_MIXERS
            C0 = jnp.zeros((bp, ML_HEADS, ML_DV, ML_DK), jnp.float32)
            n0 = jnp.zeros((bp, ML_HEADS, ML_DK), jnp.float32)
            m0 = jnp.zeros((bp, ML_HEADS), jnp.float32)
            y, C, n, m = mlstm_mixer(rmsnorm(xp, norm_g[l, 1]), ml_w_in[j], ml_b_gates[j], ml_h_norm[j], ml_w_out[j], C0, n0, m0)
            xp = xp + y
            Cp.append(C)
            np_.append(n)
            mp.append(m)
            y, C, n, m = mlstm_mixer(rmsnorm(xs, norm_g[l, 1]), ml_w_in[j], ml_b_gates[j], ml_h_norm[j], ml_w_out[j],
                                     state_C[j], state_n[j], state_m[j])
            xs = xs + y
            Cs.append(C)
            ns.append(n)
            ms.append(m)
        xp = macaron_half(xp, norm_g[l, 2], w_ffn_gate[l, 1], w_ffn_up[l, 1], w_ffn_down[l, 1])
        xs = macaron_half(xs, norm_g[l, 2], w_ffn_gate[l, 1], w_ffn_up[l, 1], w_ffn_down[l, 1])
    return (xp, xs,
            jnp.stack(kp_rows), jnp.stack(vp_rows), jnp.stack(Cp), jnp.stack(np_), jnp.stack(mp),
            jnp.stack(ks_rows), jnp.stack(vs_rows), jnp.stack(Cs), jnp.stack(ns), jnp.stack(ms))
```

```python
import functools
import math

import jax
import jax.numpy as jnp
from jax import lax
from jax.experimental import pallas as pl
from jax.experimental.pallas import tpu as pltpu

F32 = jnp.float32
BF16 = jnp.bfloat16

NORM_EPS = 1e-6
ML_CHUNK = 128
DEPTH_MIXERS = 2

V7X_VMEM_BYTES = 64 * 1024 * 1024
V7X_LANES = 128
V7X_MXU_DIM = 256
VMEM_LIMIT_BYTES = V7X_VMEM_BYTES - 6 * 1024 * 1024

MASK_NEG = -1e30
SB_KEY_BLOCK = 128
SB_ROW_CHUNK = 256


def _params(n_axes):
    return pltpu.CompilerParams(dimension_semantics=("arbitrary",) * n_axes,
                                vmem_limit_bytes=VMEM_LIMIT_BYTES)


def _largest_tile(m, cap):
    t = min(m, cap)
    while m % t:
        t -= 8
    return t


def _log_sigmoid(x):
    return jnp.minimum(x, 0.0) - jnp.log1p(jnp.exp(-jnp.abs(x)))


def _sigmoid(x):
    return 1.0 / (1.0 + jnp.exp(-x))


def _split_bf16(x):
    hi = x.astype(BF16)
    lo = (x - hi.astype(F32)).astype(BF16)
    return hi, lo


def _rmsnorm_body(x_ref, g_ref, o_ref):
    x = x_ref[...]
    ms = jnp.mean(x * x, axis=-1, keepdims=True)
    o_ref[...] = (x * lax.rsqrt(ms + NORM_EPS) * g_ref[...]).astype(o_ref.dtype)


def rmsnorm(x, g):
    m, d = x.shape
    tm = _largest_tile(m, 256)
    return pl.pallas_call(
        _rmsnorm_body,
        grid=(m // tm,),
        in_specs=[pl.BlockSpec((tm, d), lambda i: (i, 0)),
                  pl.BlockSpec((1, d), lambda i: (0, 0))],
        out_specs=pl.BlockSpec((tm, d), lambda i: (i, 0)),
        out_shape=jax.ShapeDtypeStruct((m, d), BF16),
        compiler_params=_params(1),
        name="rmsnorm",
    )(x, g.reshape(1, d))


def _epilogue(accs, extras, *, epi, coef, out_dtype):
    if epi == "swiglu":
        g, u = accs
        return (g * _sigmoid(g) * u).astype(out_dtype)
    (acc,) = accs
    if epi == "cast":
        return (acc * coef).astype(out_dtype) if coef != 1.0 else acc.astype(out_dtype)
    if epi == "bias":
        (b,) = extras
        return (acc + b).astype(out_dtype)
    if epi == "headnorm":
        (gain,) = extras
        outs = []
        for h in range(acc.shape[1] // V7X_LANES):
            a = acc[:, h * V7X_LANES:(h + 1) * V7X_LANES]
            ms = jnp.mean(a * a, axis=-1, keepdims=True)
            y = a * lax.rsqrt(ms + NORM_EPS) * gain[:, h * V7X_LANES:(h + 1) * V7X_LANES]
            outs.append(y * coef if coef != 1.0 else y)
        return jnp.concatenate(outs, axis=1).astype(out_dtype)
    if epi == "residual":
        (x,) = extras
        return (x + coef * acc).astype(out_dtype)
    raise ValueError(epi)


def _mm_body(*refs, n_w, n_row_extra, n_col_extra, epi, coef):
    ap_ref, as_ref = refs[0], refs[1]
    w_refs = refs[2:2 + n_w]
    pos = 2 + n_w
    xp_refs = refs[pos:pos + n_row_extra]
    xs_refs = refs[pos + n_row_extra:pos + 2 * n_row_extra]
    pos += 2 * n_row_extra
    c_refs = refs[pos:pos + n_col_extra]
    op_ref, os_ref = refs[pos + n_col_extra:]

    wb = [w[...].astype(BF16) for w in w_refs]
    col = [c[...] for c in c_refs]

    def run(a_ref, x_refs, o_ref):
        a = a_ref[...]
        accs = [jnp.dot(a, w, preferred_element_type=F32) for w in wb]
        extras = [x[...] for x in x_refs] + col
        o_ref[...] = _epilogue(accs, extras, epi=epi, coef=coef, out_dtype=o_ref.dtype)

    run(ap_ref, xp_refs, op_ref)

    @pl.when(pl.program_id(0) == 0)
    def _():
        run(as_ref, xs_refs, os_ref)


def matmul(ap, as_, ws, *, prefix, n_out, col_off=0, epi, coef=1.0, out_dtype,
           row_extras=(), col_extras=(), tn=V7X_MXU_DIM):
    mp, k = ap.shape
    ms = as_.shape[0]
    tm = _largest_tile(mp, 2048 if k <= 4096 else 1024)
    ni, nj = mp // tm, n_out // tn
    assert n_out % tn == 0 and mp % tm == 0
    npre = len(prefix)

    def w_map(i, j):
        return (*prefix, 0, j + col_off)

    def s_col(i, j):
        return (0, jnp.where(i == 0, j, nj - 1))

    in_specs = [pl.BlockSpec((tm, k), lambda i, j: (i, 0), pipeline_mode=pl.Buffered(1)),
                pl.BlockSpec((ms, k), lambda i, j: (0, 0))]
    in_specs += [pl.BlockSpec((None,) * npre + (k, tn), w_map) for _ in ws]
    in_specs += [pl.BlockSpec((tm, tn), lambda i, j: (i, j)) for _ in row_extras]
    in_specs += [pl.BlockSpec((ms, tn), s_col) for _ in row_extras]
    in_specs += [pl.BlockSpec((1, tn), lambda i, j: (0, j)) for _ in col_extras]
    args = [ap, as_, *ws, *[e[0] for e in row_extras], *[e[1] for e in row_extras], *col_extras]

    body = functools.partial(_mm_body, n_w=len(ws), n_row_extra=len(row_extras),
                             n_col_extra=len(col_extras), epi=epi, coef=coef)
    return pl.pallas_call(
        body,
        grid=(ni, nj),
        in_specs=in_specs,
        out_specs=[pl.BlockSpec((tm, tn), lambda i, j: (i, j)),
                   pl.BlockSpec((ms, tn), s_col)],
        out_shape=[jax.ShapeDtypeStruct((mp, n_out), out_dtype),
                   jax.ShapeDtypeStruct((ms, n_out), out_dtype)],
        compiler_params=_params(2),
        name="mm_" + epi,
    )(*args)


def _sb_prompt_body(q_ref, k_ref, v_ref, bias_ref, o_ref, kbd, vbd, acc, carry, bvar, *, t_len):
    kb_sz, rc = SB_KEY_BLOCK, SB_ROW_CHUNK
    nkb, nrc, r_per = t_len // kb_sz, t_len // rc, rc // kb_sz
    hd = V7X_LANES

    @pl.when((pl.program_id(0) == 0) & (pl.program_id(1) == 0))
    def _():
        kbd[...] = jnp.zeros_like(kbd)
        vbd[...] = jnp.zeros_like(vbd)

    for kb in range(nkb):
        rows = slice(kb * kb_sz, (kb + 1) * kb_sz)
        for e in range(2):
            cols = slice(e * hd, (e + 1) * hd)
            kbd[kb, e * kb_sz:(e + 1) * kb_sz, cols] = k_ref[rows, cols].astype(BF16)
            vbd[kb, e * kb_sz:(e + 1) * kb_sz, cols] = v_ref[rows, cols].astype(BF16)

    acc[...] = jnp.zeros_like(acc)
    carry[...] = jnp.zeros_like(carry)

    bias = bias_ref[...]
    ri = lax.broadcasted_iota(jnp.int32, (rc, 2 * kb_sz), 0)
    si = lax.broadcasted_iota(jnp.int32, (rc, 2 * kb_sz), 1) & (kb_sz - 1)
    for d in range(r_per):
        bvar[d] = jnp.where(d * kb_sz + si < ri, bias, MASK_NEG)
    bvar[r_per] = jnp.broadcast_to(bias, (rc, 2 * kb_sz))

    jr = lax.broadcasted_iota(jnp.int32, (2 * kb_sz, 4 * kb_sz), 0)
    jc = lax.broadcasted_iota(jnp.int32, (2 * kb_sz, 4 * kb_sz), 1)
    same_head = (jr // kb_sz) == ((jc // kb_sz) & 1)
    later = (jc >= 2 * kb_sz) | (jr > jc)
    tri_ones = jnp.where(same_head & later, 1.0, 0.0).astype(BF16)

    def key_block(step, _):
        kb = nkb - 1 - step
        c0 = kb // r_per
        d = kb - c0 * r_per
        kt = kbd[kb]
        vt = vbd[kb]

        def row_chunk(c, _):
            r = pl.multiple_of(c * rc, rc)
            rows = pl.ds(r, rc)
            z = lax.dot_general(q_ref[rows, :], kt, (((1,), (1,)), ((), ())),
                                preferred_element_type=F32)
            z = z + bvar[jnp.where(c == c0, d, r_per)]
            ls = _log_sigmoid(z)
            lk = ls - z
            hi, lo = _split_bf16(lk)
            cs = (jnp.dot(hi, tri_ones, preferred_element_type=F32)
                  + jnp.dot(lo, tri_ones, preferred_element_type=F32))
            w = jnp.exp(ls + cs[:, :2 * kb_sz] + carry[rows, :])
            acc[rows, :] += jnp.dot(w.astype(BF16), vt, preferred_element_type=F32)
            carry[rows, :] += cs[:, 2 * kb_sz:]
            return 0

        lax.fori_loop(c0, nrc, row_chunk, 0)
        return 0

    lax.fori_loop(0, nkb, key_block, 0)
    o_ref[...] = acc[...].astype(o_ref.dtype)


def sb_attend_prompt(q, k, v, bias, *, batch, t_len):
    m, d = q.shape
    n_pair = d // (2 * V7X_LANES)
    nkb = t_len // SB_KEY_BLOCK
    w2 = 2 * V7X_LANES
    spec = pl.BlockSpec((t_len, w2), lambda b, p: (b, p))
    return pl.pallas_call(
        functools.partial(_sb_prompt_body, t_len=t_len),
        grid=(batch, n_pair),
        in_specs=[spec, spec, spec, pl.BlockSpec((1, w2), lambda b, p: (0, p))],
        out_specs=spec,
        out_shape=jax.ShapeDtypeStruct((m, d), BF16),
        scratch_shapes=[pltpu.VMEM((nkb, w2, w2), BF16),
                        pltpu.VMEM((nkb, w2, w2), BF16),
                        pltpu.VMEM((t_len, w2), F32),
                        pltpu.VMEM((t_len, w2), F32),
                        pltpu.VMEM((SB_ROW_CHUNK // SB_KEY_BLOCK + 1, SB_ROW_CHUNK, w2), F32)],
        compiler_params=_params(2),
        name="sb_attend_prompt",
    )(q, k, v, bias.reshape(1, d // V7X_LANES * 1).repeat(V7X_LANES, axis=1))


def _sb_sample_body(pt_ref, q_ref, knew_ref, vnew_ref, kpg_ref, vpg_ref, brow_ref, o_ref,
                    qbd, acc, carry, *, n_pages, ts):
    j = pl.program_id(1)
    hd, pg = V7X_LANES, SB_KEY_BLOCK
    n_tile = qbd.shape[0]
    rows_per = 2 * ts

    jr = lax.broadcasted_iota(jnp.int32, (pg, 2 * pg), 0)
    jc = lax.broadcasted_iota(jnp.int32, (pg, 2 * pg), 1)
    tri_ones = jnp.where((jc >= pg) | (jr > jc), 1.0, 0.0).astype(BF16)

    def process(k_ref, v_ref, masked):
        zs = []
        for n in range(n_tile):
            kt = k_ref[:, n * 2 * hd:(n + 1) * 2 * hd].astype(BF16)
            zs.append(lax.dot_general(qbd[n].astype(BF16), kt, (((1,), (1,)), ((), ())),
                                      preferred_element_type=F32))
        z = jnp.concatenate(zs, axis=0) + brow_ref[...]
        if masked:
            qi = lax.broadcasted_iota(jnp.int32, z.shape, 0) % ts
            si = lax.broadcasted_iota(jnp.int32, z.shape, 1)
            z = jnp.where(si < qi, z, MASK_NEG)
        ls = _log_sigmoid(z)
        lk = ls - z
        hi, lo = _split_bf16(lk)
        cs = (jnp.dot(hi, tri_ones, preferred_element_type=F32)
              + jnp.dot(lo, tri_ones, preferred_element_type=F32))
        w = jnp.exp(ls + cs[:, :pg] + carry[...]).astype(BF16)
        for n in range(n_tile):
            vt = v_ref[:, n * 2 * hd:(n + 1) * 2 * hd].astype(BF16)
            acc[n] += jnp.dot(w[n * rows_per:(n + 1) * rows_per, :], vt,
                              preferred_element_type=F32)
        carry[...] += cs[:, pg:]

    @pl.when(j == 0)
    def _():
        qbd[...] = jnp.zeros_like(qbd)
        for n in range(n_tile):
            for e in range(2):
                qbd[n, e * ts:(e + 1) * ts, e * hd:(e + 1) * hd] = \
                    q_ref[:, (2 * n + e) * hd:(2 * n + e + 1) * hd]
        acc[...] = jnp.zeros_like(acc)
        carry[...] = jnp.zeros_like(carry)
        process(knew_ref, vnew_ref, True)

    process(kpg_ref, vpg_ref, False)

    @pl.when(j == n_pages - 1)
    def _():
        for n in range(n_tile):
            for e in range(2):
                o_ref[:, (2 * n + e) * hd:(2 * n + e + 1) * hd] = \
                    acc[n, e * ts:(e + 1) * ts, e * hd:(e + 1) * hd]


def sb_attend_sample(q, k_new, v_new, cache_k, cache_v, page_table, bias, *, layer, ts):
    m, d = q.shape
    bs = m // ts
    n_pages = page_table.shape[1]
    n_layers, n_phys, pg = cache_k.shape[:3]
    assert pg == SB_KEY_BLOCK and ts % 8 == 0
    n_tile = d // (2 * V7X_LANES)
    rows = n_tile * 2 * ts
    ck = cache_k.reshape(n_layers, n_phys, pg, d)
    cv = cache_v.reshape(n_layers, n_phys, pg, d)
    pad = lambda a: jnp.pad(a.reshape(bs, ts, d), ((0, 0), (0, pg - ts), (0, 0)))
    brow = jnp.broadcast_to(jnp.repeat(bias, ts)[:, None], (rows, pg))

    page_map = lambda b, j, pt: (layer, pt[b, n_pages - 1 - j], 0, 0)
    new_map = lambda b, j, pt: (b, 0, 0)
    grid_spec = pltpu.PrefetchScalarGridSpec(
        num_scalar_prefetch=1,
        grid=(bs, n_pages),
        in_specs=[pl.BlockSpec((ts, d), lambda b, j, pt: (b, 0)),
                  pl.BlockSpec((None, pg, d), new_map),
                  pl.BlockSpec((None, pg, d), new_map),
                  pl.BlockSpec((None, None, pg, d), page_map),
                  pl.BlockSpec((None, None, pg, d), page_map),
                  pl.BlockSpec((rows, pg), lambda b, j, pt: (0, 0))],
        out_specs=pl.BlockSpec((ts, d), lambda b, j, pt: (b, 0)),
        scratch_shapes=[pltpu.VMEM((n_tile, 2 * ts, 2 * V7X_LANES), F32),
                        pltpu.VMEM((n_tile, 2 * ts, 2 * V7X_LANES), F32),
                        pltpu.VMEM((rows, pg), F32)],
    )
    return pl.pallas_call(
        functools.partial(_sb_sample_body, n_pages=n_pages, ts=ts),
        grid_spec=grid_spec,
        out_shape=jax.ShapeDtypeStruct((m, d), F32),
        compiler_params=_params(2),
        name="sb_attend_sample",
    )(page_table, q, pad(k_new), pad(v_new), ck, cv, brow)


def _mxu_operand(x):
    y = x.astype(BF16)
    return y if x.shape[0] >= 16 else y.astype(F32)


def _mlstm_body(q_ref, k_ref, v_ref, o_ref, ir_ref, ic_ref, fr_ref, fc_ref, hg_ref,
                c0_ref, n0_ref, m0_ref, y_ref, c_ref, n_ref, m_ref, ct, ns, ms, *, chunk, n_chunks):
    c = pl.program_id(2)
    ln = chunk

    @pl.when(c == 0)
    def _():
        ct[...] = c0_ref[...].T
        ns[...] = n0_ref[...]
        ms[...] = m0_ref[...]

    q = _mxu_operand(q_ref[...])
    k = _mxu_operand(k_ref[...])
    v = _mxu_operand(v_ref[...])
    lf_r = _log_sigmoid(fr_ref[...])
    lf_c = _log_sigmoid(fc_ref[...])
    ti = lax.broadcasted_iota(jnp.int32, (ln, ln), 0)
    si = lax.broadcasted_iota(jnp.int32, (ln, ln), 1)
    causal = si <= ti
    b_col = jnp.sum(jnp.where(causal, lf_r, 0.0), axis=1, keepdims=True)
    b_row = jnp.sum(jnp.where(ti <= si, lf_c, 0.0), axis=0, keepdims=True)
    dmat = jnp.where(causal, b_col - b_row + ir_ref[...], -jnp.inf)
    m_prev = ms[...]
    g = b_col + m_prev
    m_t = jnp.maximum(g, jnp.max(dmat, axis=1, keepdims=True))
    s = lax.dot_general(q, k, (((1,), (1,)), ((), ())), preferred_element_type=F32)
    w = jnp.exp(dmat - m_t) * s
    w_carry = jnp.exp(g - m_t)
    num = (w_carry * jnp.dot(q, _mxu_operand(ct[...]), preferred_element_type=F32)
           + jnp.dot(_mxu_operand(w), v, preferred_element_type=F32))
    qn = jnp.sum(q.astype(F32) * ns[...], axis=1, keepdims=True)
    den = w_carry * qn + jnp.sum(w, axis=1, keepdims=True)
    h = num / jnp.maximum(jnp.abs(den), jnp.exp(-m_t))

    hms = jnp.mean(h * h, axis=1, keepdims=True)
    hn = h * lax.rsqrt(hms + NORM_EPS) * hg_ref[...]
    y_ref[...] = (_sigmoid(o_ref[...]) * hn).astype(y_ref.dtype)

    m_new = m_t[ln - 1:ln, :]
    b_last = b_col[ln - 1:ln, :]
    decay = jnp.exp(b_last - b_col + ic_ref[...] - m_new)
    c_scale = jnp.exp(b_last + m_prev - m_new)
    kd = decay * k.astype(F32)
    ct[...] = c_scale * ct[...] + lax.dot_general(
        _mxu_operand(kd), v, (((0,), (0,)), ((), ())), preferred_element_type=F32)
    ns[...] = c_scale * ns[...] + jnp.sum(kd, axis=0, keepdims=True)
    ms[...] = m_new

    @pl.when(c == n_chunks - 1)
    def _():
        c_ref[...] = ct[...].T
        n_ref[...] = ns[...]
        m_ref[...] = ms[...]


def mlstm(q, k, v, o, gates, h_gain, c0, n0, m0, *, batch, t_len, y_dtype):
    heads, dv, dk = c0.shape[1:]
    chunk = ML_CHUNK if t_len % ML_CHUNK == 0 else t_len
    nc = t_len // chunk
    gt = gates[:, :2 * heads].reshape(batch, nc, chunk, 2, heads)
    gt = jnp.transpose(gt, (3, 0, 4, 1, 2))
    i_row, f_row = gt[0][..., None, :], gt[1][..., None, :]
    i_col, f_col = gt[0][..., :, None], gt[1][..., :, None]

    tok = lambda width: pl.BlockSpec((chunk, width), lambda b, h, c: (b * nc + c, h))
    row = pl.BlockSpec((None, None, None, 1, chunk), lambda b, h, c: (b, h, c, 0, 0))
    col = pl.BlockSpec((None, None, None, chunk, 1), lambda b, h, c: (b, h, c, 0, 0))
    st = lambda r, w: pl.BlockSpec((None, None, r, w), lambda b, h, c: (b, h, 0, 0))
    return pl.pallas_call(
        functools.partial(_mlstm_body, chunk=chunk, n_chunks=nc),
        grid=(batch, heads, nc),
        in_specs=[tok(dk), tok(dk), tok(dv), tok(dv), row, col, row, col,
                  pl.BlockSpec((1, dv), lambda b, h, c: (0, h)),
                  st(dv, dk), st(1, dk), st(1, 1)],
        out_specs=[tok(dv), st(dv, dk), st(1, dk), st(1, 1)],
        out_shape=[jax.ShapeDtypeStruct((batch * t_len, heads * dv), y_dtype),
                   jax.ShapeDtypeStruct((batch, heads, dv, dk), F32),
                   jax.ShapeDtypeStruct((batch, heads, 1, dk), F32),
                   jax.ShapeDtypeStruct((batch, heads, 1, 1), F32)],
        scratch_shapes=[pltpu.VMEM((dk, dv), F32), pltpu.VMEM((1, dk), F32), pltpu.VMEM((1, 1), F32)],
        compiler_params=_params(3),
        name="mlstm",
    )(q, k, v, o, i_row, i_col, f_row, f_col, h_gain.reshape(1, heads * dv),
      c0, n0.reshape(batch, heads, 1, dk), m0.reshape(batch, heads, 1, 1))


def _ffn_half(xp, xs, gain, w_gate, w_up, w_down_bf16, prefix):
    hp, hs = rmsnorm(xp, gain), rmsnorm(xs, gain)
    d_ff = w_gate.shape[-1]
    ap, as_ = matmul(hp, hs, (w_gate, w_up), prefix=prefix, n_out=d_ff, epi="swiglu", out_dtype=BF16)
    return matmul(ap, as_, (w_down_bf16,), prefix=prefix, n_out=xp.shape[1], epi="residual", coef=0.5,
                  out_dtype=F32, row_extras=((xp, xs),))


def kernel(x_prompt, x_sample, cache_k, cache_v, state_C, state_n, state_m, page_table, norm_g, w_ffn_gate, w_ffn_up, w_ffn_down, sb_w_in, sb_q_norm, sb_k_norm, sb_logit_bias, sb_w_out, ml_w_in, ml_b_gates, ml_h_norm, ml_w_out):
    bp, t_len, d = x_prompt.shape
    bs, ts, _ = x_sample.shape
    depth = norm_g.shape[0]
    sb_heads = sb_logit_bias.shape[1]
    sb_hd = d // sb_heads
    assert sb_hd == V7X_LANES
    ml_heads, ml_dv, ml_dk = state_C.shape[2:]
    tn = V7X_MXU_DIM

    xp = x_prompt.reshape(bp * t_len, d)
    xs = x_sample.reshape(bs * ts, d)
    w_down = w_ffn_down.astype(BF16)
    kp_rows, vp_rows, ks_rows, vs_rows = [], [], [], []
    c_p, n_p, m_p, c_s, n_s, m_s = [], [], [], [], [], []

    for l in range(depth):
        xp, xs = _ffn_half(xp, xs, norm_g[l, 0], w_ffn_gate, w_ffn_up, w_down, (l, 0))
        hp, hs = rmsnorm(xp, norm_g[l, 1]), rmsnorm(xs, norm_g[l, 1])
        if l % DEPTH_MIXERS == 0:
            a = l // DEPTH_MIXERS
            tile = lambda g: jnp.tile(g, d // sb_hd).reshape(1, d)
            qp, qs = matmul(hp, hs, (sb_w_in,), prefix=(a,), n_out=d, col_off=0, epi="headnorm",
                            coef=1.0 / math.sqrt(sb_hd), out_dtype=BF16, col_extras=(tile(sb_q_norm[a]),))
            kp, ks = matmul(hp, hs, (sb_w_in,), prefix=(a,), n_out=d, col_off=d // tn, epi="headnorm",
                            out_dtype=F32, col_extras=(tile(sb_k_norm[a]),))
            vp, vs = matmul(hp, hs, (sb_w_in,), prefix=(a,), n_out=d, col_off=2 * d // tn, epi="cast",
                            out_dtype=F32)
            att_p = sb_attend_prompt(qp, kp, vp, sb_logit_bias[a], batch=bp, t_len=t_len)
            att_s = sb_attend_sample(qs.astype(F32), ks, vs, cache_k, cache_v, page_table,
                                     sb_logit_bias[a], layer=a, ts=ts)
            xp, xs = matmul(att_p, att_s.astype(BF16), (sb_w_out,), prefix=(a,), n_out=d, epi="residual",
                            out_dtype=F32, row_extras=((xp, xs),))
            kp_rows.append(kp.reshape(bp, t_len, sb_heads, sb_hd))
            vp_rows.append(vp.reshape(bp, t_len, sb_heads, sb_hd))
            ks_rows.append(ks.reshape(bs, ts, sb_heads, sb_hd))
            vs_rows.append(vs.reshape(bs, ts, sb_heads, sb_hd))
        else:
            jl = l // DEPTH_MIXERS
            s_qk = ml_heads * ml_dk
            proj = functools.partial(matmul, hp, hs, (ml_w_in,), prefix=(jl,))
            qp, qs = proj(n_out=s_qk, col_off=0, epi="cast", out_dtype=BF16)
            kp, ks = proj(n_out=s_qk, col_off=s_qk // tn, epi="cast", coef=ml_dk ** -0.5, out_dtype=BF16)
            vp, vs = proj(n_out=d, col_off=2 * s_qk // tn, epi="cast", out_dtype=BF16)
            op, os_ = proj(n_out=d, col_off=(2 * s_qk + d) // tn, epi="cast", out_dtype=F32)
            n_gate = 2 * ml_heads
            w_g = jnp.pad(ml_w_in[jl:jl + 1, :, 2 * s_qk + 2 * d:], ((0, 0), (0, 0), (0, V7X_LANES - n_gate)))
            b_g = jnp.pad(ml_b_gates[jl].reshape(1, n_gate), ((0, 0), (0, V7X_LANES - n_gate)))
            gp, gs = matmul(hp, hs, (w_g,), prefix=(0,), n_out=V7X_LANES, epi="bias", out_dtype=F32,
                            col_extras=(b_g,), tn=V7X_LANES)

            zeros = lambda *shape: jnp.zeros(shape, F32)
            yp, c, n, m = mlstm(qp, kp, vp, op, gp, ml_h_norm[jl], zeros(bp, ml_heads, ml_dv, ml_dk),
                                zeros(bp, ml_heads, ml_dk), zeros(bp, ml_heads),
                                batch=bp, t_len=t_len, y_dtype=BF16)
            c_p.append(c); n_p.append(n.reshape(bp, ml_heads, ml_dk)); m_p.append(m.reshape(bp, ml_heads))
            f32 = lambda a_: a_.astype(F32)
            ys, c, n, m = mlstm(f32(qs), f32(ks), f32(vs), os_, gs, ml_h_norm[jl], state_C[jl], state_n[jl],
                                state_m[jl], batch=bs, t_len=ts, y_dtype=F32)
            c_s.append(c); n_s.append(n.reshape(bs, ml_heads, ml_dk)); m_s.append(m.reshape(bs, ml_heads))
            xp, xs = matmul(yp, ys.astype(BF16), (ml_w_out,), prefix=(jl,), n_out=d, epi="residual",
                            out_dtype=F32, row_extras=((xp, xs),))
        xp, xs = _ffn_half(xp, xs, norm_g[l, 2], w_ffn_gate, w_ffn_up, w_down, (l, 1))

    stack = lambda rows: rows[0][None] if len(rows) == 1 else jnp.stack(rows)
    return (xp.reshape(bp, t_len, d), xs.reshape(bs, ts, d),
            stack(kp_rows), stack(vp_rows), stack(c_p), stack(n_p), stack(m_p),
            stack(ks_rows), stack(vs_rows), stack(c_s), stack(n_s), stack(m_s))
```

```python
import functools
import math

import jax
import jax.numpy as jnp
from jax import lax
from jax.experimental import pallas as pl
from jax.experimental.pallas import tpu as pltpu

F32 = jnp.float32
BF16 = jnp.bfloat16

NORM_EPS = 1e-6
ML_CHUNK = 128
DEPTH_MIXERS = 2

V7X_VMEM_BYTES = 64 * 1024 * 1024
V7X_LANES = 128
V7X_MXU_DIM = 256
VMEM_LIMIT_BYTES = V7X_VMEM_BYTES - 6 * 1024 * 1024

LOG2E = math.log2(math.e)
MASK_NEG = -1e30
SB_KEY_BLOCK = 128
SB_ROW_CHUNK = 256


def _params(n_axes):
    return pltpu.CompilerParams(dimension_semantics=("arbitrary",) * n_axes,
                                vmem_limit_bytes=VMEM_LIMIT_BYTES)


def _largest_tile(m, cap):
    t = min(m, cap)
    while m % t:
        t -= 8
    return t


def _log_sigmoid(x):
    return jnp.minimum(x, 0.0) - jnp.log(1.0 + jnp.exp(-jnp.abs(x)))


def _log2_sigmoid(x2):
    neg_abs = lax.bitcast_convert_type(lax.bitcast_convert_type(x2, jnp.uint32) | jnp.uint32(1 << 31), F32)
    return jnp.minimum(x2, 0.0) - jnp.log2(1.0 + jnp.exp2(neg_abs))


def _sigmoid(x):
    return 1.0 / (1.0 + jnp.exp(-x))


def _split_bf16(x):
    hi = x.astype(BF16)
    lo = (x - hi.astype(F32)).astype(BF16)
    return hi, lo


def _rmsnorm_body(x_ref, g_ref, o_ref):
    x = x_ref[...]
    ms = jnp.mean(x * x, axis=-1, keepdims=True)
    o_ref[...] = (x * lax.rsqrt(ms + NORM_EPS) * g_ref[...]).astype(o_ref.dtype)


def rmsnorm(x, g):
    m, d = x.shape
    tm = _largest_tile(m, 256)
    return pl.pallas_call(
        _rmsnorm_body,
        grid=(m // tm,),
        in_specs=[pl.BlockSpec((tm, d), lambda i: (i, 0)),
                  pl.BlockSpec((1, d), lambda i: (0, 0))],
        out_specs=pl.BlockSpec((tm, d), lambda i: (i, 0)),
        out_shape=jax.ShapeDtypeStruct((m, d), BF16),
        compiler_params=_params(1),
        name="rmsnorm",
    )(x, g.reshape(1, d))


def _epilogue(accs, extras, *, epi, coef, out_dtype):
    if epi == "swiglu":
        g, u = accs
        return (g * _sigmoid(g) * u).astype(out_dtype)
    (acc,) = accs
    if epi == "cast":
        return (acc * coef).astype(out_dtype) if coef != 1.0 else acc.astype(out_dtype)
    if epi == "bias":
        (b,) = extras
        return (acc + b).astype(out_dtype)
    if epi == "headnorm":
        (gain,) = extras
        outs = []
        for h in range(acc.shape[1] // V7X_LANES):
            a = acc[:, h * V7X_LANES:(h + 1) * V7X_LANES]
            ms = jnp.mean(a * a, axis=-1, keepdims=True)
            y = a * lax.rsqrt(ms + NORM_EPS) * gain[:, h * V7X_LANES:(h + 1) * V7X_LANES]
            outs.append(y * coef if coef != 1.0 else y)
        return jnp.concatenate(outs, axis=1).astype(out_dtype)
    if epi == "residual":
        (x,) = extras
        return (x + coef * acc).astype(out_dtype)
    raise ValueError(epi)


def _mm_body(*refs, n_w, n_row_extra, n_col_extra, epi, coef):
    ap_ref, as_ref = refs[0], refs[1]
    w_refs = refs[2:2 + n_w]
    pos = 2 + n_w
    xp_refs = refs[pos:pos + n_row_extra]
    xs_refs = refs[pos + n_row_extra:pos + 2 * n_row_extra]
    pos += 2 * n_row_extra
    c_refs = refs[pos:pos + n_col_extra]
    op_ref, os_ref = refs[pos + n_col_extra:]

    wb = [w[...].astype(BF16) for w in w_refs]
    col = [c[...] for c in c_refs]

    def run(a_ref, x_refs, o_ref):
        a = a_ref[...]
        accs = [jnp.dot(a, w, preferred_element_type=F32) for w in wb]
        extras = [x[...] for x in x_refs] + col
        o_ref[...] = _epilogue(accs, extras, epi=epi, coef=coef, out_dtype=o_ref.dtype)

    run(ap_ref, xp_refs, op_ref)

    @pl.when(pl.program_id(0) == 0)
    def _():
        run(as_ref, xs_refs, os_ref)


def matmul(ap, as_, ws, *, prefix, n_out, col_off=0, epi, coef=1.0, out_dtype,
           row_extras=(), col_extras=(), tn=V7X_MXU_DIM):
    mp, k = ap.shape
    ms = as_.shape[0]
    tm = _largest_tile(mp, 2048 if k <= 4096 else 1024)
    ni, nj = mp // tm, n_out // tn
    assert n_out % tn == 0 and mp % tm == 0
    npre = len(prefix)

    def w_map(i, j):
        return (*prefix, 0, j + col_off)

    def s_col(i, j):
        return (0, jnp.where(i == 0, j, nj - 1))

    in_specs = [pl.BlockSpec((tm, k), lambda i, j: (i, 0), pipeline_mode=pl.Buffered(1)),
                pl.BlockSpec((ms, k), lambda i, j: (0, 0))]
    in_specs += [pl.BlockSpec((None,) * npre + (k, tn), w_map) for _ in ws]
    in_specs += [pl.BlockSpec((tm, tn), lambda i, j: (i, j)) for _ in row_extras]
    in_specs += [pl.BlockSpec((ms, tn), s_col) for _ in row_extras]
    in_specs += [pl.BlockSpec((1, tn), lambda i, j: (0, j)) for _ in col_extras]
    args = [ap, as_, *ws, *[e[0] for e in row_extras], *[e[1] for e in row_extras], *col_extras]

    body = functools.partial(_mm_body, n_w=len(ws), n_row_extra=len(row_extras),
                             n_col_extra=len(col_extras), epi=epi, coef=coef)
    return pl.pallas_call(
        body,
        grid=(ni, nj),
        in_specs=in_specs,
        out_specs=[pl.BlockSpec((tm, tn), lambda i, j: (i, j)),
                   pl.BlockSpec((ms, tn), s_col)],
        out_shape=[jax.ShapeDtypeStruct((mp, n_out), out_dtype),
                   jax.ShapeDtypeStruct((ms, n_out), out_dtype)],
        compiler_params=_params(2),
        name="mm_" + epi,
    )(*args)


SB_UNROLL = 4


def _sb_schedule(t_len):
    kb_sz, rc = SB_KEY_BLOCK, SB_ROW_CHUNK
    r_per = rc // kb_sz
    items = [(kb, c) for kb in reversed(range(t_len // kb_sz)) for c in range(kb // r_per, t_len // rc)]
    n_trip = -(-(len(items) + 5) // SB_UNROLL) * SB_UNROLL
    spare = (0, t_len, r_per)

    def entry(i):
        if not 0 <= i < len(items):
            return spare
        kb, c = items[i]
        return kb, c * rc, (kb - c * r_per) if c == kb // r_per else r_per

    cols = [[entry(t - lag)[f] for t in range(n_trip)] for lag, f in
            ((0, 0), (0, 1), (1, 2), (3, 1), (4, 0), (5, 1))]
    cols[1] = [min(r, t_len - rc) for r in cols[1]]
    return n_trip, jnp.asarray(cols, jnp.int32)


def _sb_prompt_body(tbl_ref, q_ref, k_ref, v_ref, bias_ref, o_ref,
                    ktd, vbd, acc, carry, bvar, z_raw, s_ls, s_hl, cs_raw, s_w, o_raw, *, t_len, n_trip):
    kb_sz, rc = SB_KEY_BLOCK, SB_ROW_CHUNK
    nkb, r_per = t_len // kb_sz, rc // kb_sz
    hd = V7X_LANES

    @pl.when((pl.program_id(0) == 0) & (pl.program_id(1) == 0))
    def _():
        ktd[...] = jnp.zeros_like(ktd)
        vbd[...] = jnp.zeros_like(vbd)

    for kb in range(nkb):
        rows = slice(kb * kb_sz, (kb + 1) * kb_sz)
        for e in range(2):
            cols = slice(e * hd, (e + 1) * hd)
            ktd[kb, e * hd:(e + 1) * hd, e * kb_sz:(e + 1) * kb_sz] = k_ref[rows, cols].T.astype(BF16)
            vbd[kb, e * kb_sz:(e + 1) * kb_sz, cols] = v_ref[rows, cols].astype(BF16)

    for buf in (acc, carry, z_raw, s_ls, s_hl, cs_raw, s_w, o_raw):
        buf[...] = jnp.zeros_like(buf)

    bias = bias_ref[...]
    ri = lax.broadcasted_iota(jnp.int32, (rc, 2 * kb_sz), 0)
    si = lax.broadcasted_iota(jnp.int32, (rc, 2 * kb_sz), 1) & (kb_sz - 1)
    for d in range(r_per):
        bvar[d] = jnp.where(d * kb_sz + si < ri, bias, MASK_NEG)
    bvar[r_per] = jnp.broadcast_to(bias, (rc, 2 * kb_sz))

    jr = lax.broadcasted_iota(jnp.int32, (4 * kb_sz, 4 * kb_sz), 0) & (2 * kb_sz - 1)
    jc = lax.broadcasted_iota(jnp.int32, (4 * kb_sz, 4 * kb_sz), 1)
    same_head = (jr // kb_sz) == ((jc // kb_sz) & 1)
    later = (jc >= 2 * kb_sz) | (jr > jc)
    tri_ones = jnp.where(same_head & later, 1.0, 0.0).astype(BF16)

    def trip(t, phase):
        cur, prev = phase % 2, 1 - phase % 2
        rows = lambda r: pl.ds(pl.multiple_of(tbl_ref[r, t], rc), rc)
        z_raw[cur] = jnp.dot(q_ref[rows(1), :], ktd[tbl_ref[0, t]], preferred_element_type=F32)
        z = z_raw[prev] + bvar[tbl_ref[2, t]]
        ls = _log2_sigmoid(z)
        hi, lo = _split_bf16(ls - z)
        s_ls[phase] = ls
        s_hl[cur] = jnp.concatenate([hi, lo], axis=1)
        cs_raw[cur] = jnp.dot(s_hl[prev], tri_ones, preferred_element_type=F32)
        rows_e2 = rows(3)
        cs = cs_raw[prev]
        s_w[cur] = jnp.exp2(s_ls[(phase - 2) % SB_UNROLL] + cs[:, :2 * kb_sz] + carry[rows_e2, :]).astype(BF16)
        carry[rows_e2, :] += cs[:, 2 * kb_sz:]
        o_raw[cur] = jnp.dot(s_w[prev], vbd[tbl_ref[4, t]], preferred_element_type=F32)
        acc[rows(5), :] += o_raw[prev]

    def trips(u, _):
        for phase in range(SB_UNROLL):
            trip(SB_UNROLL * u + phase, phase)
        return 0

    lax.fori_loop(0, n_trip // SB_UNROLL, trips, 0)
    o_ref[...] = acc[0:t_len, :].astype(o_ref.dtype)


def sb_attend_prompt(q, k, v, bias, *, batch, t_len):
    m, d = q.shape
    w2 = 2 * V7X_LANES
    n_pair = d // w2
    nkb = t_len // SB_KEY_BLOCK
    rc = SB_ROW_CHUNK
    assert t_len % rc == 0
    n_trip, tables = _sb_schedule(t_len)
    spec = pl.BlockSpec((t_len, w2), lambda b, p, tbl: (b, p))
    grid_spec = pltpu.PrefetchScalarGridSpec(
        num_scalar_prefetch=1,
        grid=(batch, n_pair),
        in_specs=[spec, spec, spec, pl.BlockSpec((1, w2), lambda b, p, tbl: (0, p))],
        out_specs=spec,
        scratch_shapes=[pltpu.VMEM((nkb, w2, w2), BF16),
                        pltpu.VMEM((nkb, w2, w2), BF16),
                        pltpu.VMEM((t_len + rc, w2), F32),
                        pltpu.VMEM((t_len + rc, w2), F32),
                        pltpu.VMEM((rc // SB_KEY_BLOCK + 1, rc, w2), F32),
                        pltpu.VMEM((2, rc, w2), F32),
                        pltpu.VMEM((SB_UNROLL, rc, w2), F32),
                        pltpu.VMEM((2, rc, 2 * w2), BF16),
                        pltpu.VMEM((2, rc, 2 * w2), F32),
                        pltpu.VMEM((2, rc, w2), BF16),
                        pltpu.VMEM((2, rc, w2), F32)],
    )
    return pl.pallas_call(
        functools.partial(_sb_prompt_body, t_len=t_len, n_trip=n_trip),
        grid_spec=grid_spec,
        out_shape=jax.ShapeDtypeStruct((m, d), BF16),
        compiler_params=_params(2),
        name="sb_attend_prompt",
    )(tables, q, k, v, jnp.repeat(bias * LOG2E, V7X_LANES).reshape(1, d))


def _sb_sample_body(pt_ref, q_ref, knew_ref, vnew_ref, kpg_ref, vpg_ref, brow_ref, o_ref,
                    qbd, acc, carry, khm, vhm, *, n_pages, ts):
    j = pl.program_id(1)
    hd, pg = V7X_LANES, SB_KEY_BLOCK
    n_tile = qbd.shape[0]
    rows_per = 2 * ts

    jr = lax.broadcasted_iota(jnp.int32, (2 * pg, 2 * pg), 0) & (pg - 1)
    jc = lax.broadcasted_iota(jnp.int32, (2 * pg, 2 * pg), 1)
    tri_ones = jnp.where((jc >= pg) | (jr > jc), 1.0, 0.0).astype(BF16)

    def process(k_ref, v_ref, masked):
        khm[...] = jnp.swapaxes(k_ref[...], 0, 1).astype(BF16)
        vhm[...] = jnp.swapaxes(v_ref[...], 0, 1).astype(BF16)
        zs = []
        for n in range(n_tile):
            kt = jnp.concatenate([khm[2 * n], khm[2 * n + 1]], axis=1)
            zs.append(lax.dot_general(qbd[n].astype(BF16), kt, (((1,), (1,)), ((), ())),
                                      preferred_element_type=F32))
        z = jnp.concatenate(zs, axis=0) + brow_ref[...]
        if masked:
            qi = lax.broadcasted_iota(jnp.int32, z.shape, 0) % ts
            si = lax.broadcasted_iota(jnp.int32, z.shape, 1)
            z = jnp.where(si < qi, z, MASK_NEG)
        ls = _log2_sigmoid(z)
        hi, lo = _split_bf16(ls - z)
        cs = jnp.dot(jnp.concatenate([hi, lo], axis=1), tri_ones,
                     preferred_element_type=F32)
        w = jnp.exp2(ls + cs[:, :pg] + carry[...]).astype(BF16)
        for n in range(n_tile):
            vt = jnp.concatenate([vhm[2 * n], vhm[2 * n + 1]], axis=1)
            acc[n] += jnp.dot(w[n * rows_per:(n + 1) * rows_per, :], vt,
                              preferred_element_type=F32)
        carry[...] += cs[:, pg:]

    @pl.when(j == 0)
    def _():
        qbd[...] = jnp.zeros_like(qbd)
        for n in range(n_tile):
            for e in range(2):
                qbd[n, e * ts:(e + 1) * ts, e * hd:(e + 1) * hd] = \
                    q_ref[:, (2 * n + e) * hd:(2 * n + e + 1) * hd]
        acc[...] = jnp.zeros_like(acc)
        carry[...] = jnp.zeros_like(carry)
        process(knew_ref, vnew_ref, True)

    process(kpg_ref, vpg_ref, False)

    @pl.when(j == n_pages - 1)
    def _():
        for n in range(n_tile):
            for e in range(2):
                o_ref[:, (2 * n + e) * hd:(2 * n + e + 1) * hd] = \
                    acc[n, e * ts:(e + 1) * ts, e * hd:(e + 1) * hd]


def sb_attend_sample(q, k_new, v_new, cache_k, cache_v, page_table, bias, *, layer, ts):
    m, d = q.shape
    bs = m // ts
    n_pages = page_table.shape[1]
    n_layers, n_phys, pg = cache_k.shape[:3]
    assert pg == SB_KEY_BLOCK and ts % 8 == 0
    heads, hd = cache_k.shape[3:]
    assert hd == V7X_LANES and heads * hd == d
    n_tile = heads // 2
    rows = heads * ts
    pad = lambda a: jnp.pad(a.reshape(bs, ts, heads, hd), ((0, 0), (0, pg - ts), (0, 0), (0, 0)))
    brow = jnp.broadcast_to(jnp.repeat(bias * LOG2E, ts)[:, None], (rows, pg))

    page_map = lambda b, j, pt: (layer, pt[b, n_pages - 1 - j], 0, 0, 0)
    new_map = lambda b, j, pt: (b, 0, 0, 0)
    grid_spec = pltpu.PrefetchScalarGridSpec(
        num_scalar_prefetch=1,
        grid=(bs, n_pages),
        in_specs=[pl.BlockSpec((ts, d), lambda b, j, pt: (b, 0)),
                  pl.BlockSpec((None, pg, heads, hd), new_map),
                  pl.BlockSpec((None, pg, heads, hd), new_map),
                  pl.BlockSpec((None, None, pg, heads, hd), page_map),
                  pl.BlockSpec((None, None, pg, heads, hd), page_map),
                  pl.BlockSpec((rows, pg), lambda b, j, pt: (0, 0))],
        out_specs=pl.BlockSpec((ts, d), lambda b, j, pt: (b, 0)),
        scratch_shapes=[pltpu.VMEM((n_tile, 2 * ts, 2 * hd), F32),
                        pltpu.VMEM((n_tile, 2 * ts, 2 * hd), F32),
                        pltpu.VMEM((rows, pg), F32),
                        pltpu.VMEM((heads, pg, hd), BF16),
                        pltpu.VMEM((heads, pg, hd), BF16)],
    )
    return pl.pallas_call(
        functools.partial(_sb_sample_body, n_pages=n_pages, ts=ts),
        grid_spec=grid_spec,
        out_shape=jax.ShapeDtypeStruct((m, d), F32),
        compiler_params=_params(2),
        name="sb_attend_sample",
    )(page_table, q, pad(k_new), pad(v_new), cache_k, cache_v, brow)


def _mxu_operand(x):
    y = x.astype(BF16)
    return y if x.shape[0] >= 16 else y.astype(F32)


def _mlstm_body(q_ref, k_ref, v_ref, o_ref, ir_ref, ic_ref, fr_ref, fc_ref, hg_ref,
                c0_ref, n0_ref, m0_ref, y_ref, c_ref, n_ref, m_ref, ct, ns, ms, *, chunk, n_chunks):
    c = pl.program_id(2)
    ln = chunk

    @pl.when(c == 0)
    def _():
        ct[...] = c0_ref[...].T
        ns[...] = n0_ref[...]
        ms[...] = m0_ref[...]

    q = _mxu_operand(q_ref[...])
    k = _mxu_operand(k_ref[...])
    v = _mxu_operand(v_ref[...])
    lf_r = _log_sigmoid(fr_ref[...])
    lf_c = _log_sigmoid(fc_ref[...])
    ti = lax.broadcasted_iota(jnp.int32, (ln, ln), 0)
    si = lax.broadcasted_iota(jnp.int32, (ln, ln), 1)
    causal = si <= ti
    b_col = jnp.sum(jnp.where(causal, lf_r, 0.0), axis=1, keepdims=True)
    b_row = jnp.sum(jnp.where(ti <= si, lf_c, 0.0), axis=0, keepdims=True)
    dmat = jnp.where(causal, b_col - b_row + ir_ref[...], -jnp.inf)
    m_prev = ms[...]
    g = b_col + m_prev
    m_t = jnp.maximum(g, jnp.max(dmat, axis=1, keepdims=True))
    s = lax.dot_general(q, k, (((1,), (1,)), ((), ())), preferred_element_type=F32)
    w = jnp.exp(dmat - m_t) * s
    w_carry = jnp.exp(g - m_t)
    num = (w_carry * jnp.dot(q, _mxu_operand(ct[...]), preferred_element_type=F32)
           + jnp.dot(_mxu_operand(w), v, preferred_element_type=F32))
    qn = jnp.sum(q.astype(F32) * ns[...], axis=1, keepdims=True)
    den = w_carry * qn + jnp.sum(w, axis=1, keepdims=True)
    h = num / jnp.maximum(jnp.abs(den), jnp.exp(-m_t))

    hms = jnp.mean(h * h, axis=1, keepdims=True)
    hn = h * lax.rsqrt(hms + NORM_EPS) * hg_ref[...]
    y_ref[...] = (_sigmoid(o_ref[...]) * hn).astype(y_ref.dtype)

    m_new = m_t[ln - 1:ln, :]
    b_last = b_col[ln - 1:ln, :]
    decay = jnp.exp(b_last - b_col + ic_ref[...] - m_new)
    c_scale = jnp.exp(b_last + m_prev - m_new)
    kd = decay * k.astype(F32)
    ct[...] = c_scale * ct[...] + lax.dot_general(
        _mxu_operand(kd), v, (((0,), (0,)), ((), ())), preferred_element_type=F32)
    ns[...] = c_scale * ns[...] + jnp.sum(kd, axis=0, keepdims=True)
    ms[...] = m_new

    @pl.when(c == n_chunks - 1)
    def _():
        c_ref[...] = ct[...].T
        n_ref[...] = ns[...]
        m_ref[...] = ms[...]


def mlstm(q, k, v, o, gates, h_gain, c0, n0, m0, *, batch, t_len, y_dtype):
    heads, dv, dk = c0.shape[1:]
    chunk = ML_CHUNK if t_len % ML_CHUNK == 0 else t_len
    nc = t_len // chunk
    gt = gates[:, :2 * heads].reshape(batch, nc, chunk, 2, heads)
    gt = jnp.transpose(gt, (3, 0, 4, 1, 2))
    i_row, f_row = gt[0][..., None, :], gt[1][..., None, :]
    i_col, f_col = gt[0][..., :, None], gt[1][..., :, None]

    tok = lambda width: pl.BlockSpec((chunk, width), lambda b, h, c: (b * nc + c, h))
    row = pl.BlockSpec((None, None, None, 1, chunk), lambda b, h, c: (b, h, c, 0, 0))
    col = pl.BlockSpec((None, None, None, chunk, 1), lambda b, h, c: (b, h, c, 0, 0))
    st = lambda r, w: pl.BlockSpec((None, None, r, w), lambda b, h, c: (b, h, 0, 0))
    return pl.pallas_call(
        functools.partial(_mlstm_body, chunk=chunk, n_chunks=nc),
        grid=(batch, heads, nc),
        in_specs=[tok(dk), tok(dk), tok(dv), tok(dv), row, col, row, col,
                  pl.BlockSpec((1, dv), lambda b, h, c: (0, h)),
                  st(dv, dk), st(1, dk), st(1, 1)],
        out_specs=[tok(dv), st(dv, dk), st(1, dk), st(1, 1)],
        out_shape=[jax.ShapeDtypeStruct((batch * t_len, heads * dv), y_dtype),
                   jax.ShapeDtypeStruct((batch, heads, dv, dk), F32),
                   jax.ShapeDtypeStruct((batch, heads, 1, dk), F32),
                   jax.ShapeDtypeStruct((batch, heads, 1, 1), F32)],
        scratch_shapes=[pltpu.VMEM((dk, dv), F32), pltpu.VMEM((1, dk), F32), pltpu.VMEM((1, 1), F32)],
        compiler_params=_params(3),
        name="mlstm",
    )(q, k, v, o, i_row, i_col, f_row, f_col, h_gain.reshape(1, heads * dv),
      c0, n0.reshape(batch, heads, 1, dk), m0.reshape(batch, heads, 1, 1))


def _ffn_half(xp, xs, gain, w_gate, w_up, w_down, prefix):
    hp, hs = rmsnorm(xp, gain), rmsnorm(xs, gain)
    d_ff = w_gate.shape[-1]
    ap, as_ = matmul(hp, hs, (w_gate, w_up), prefix=prefix, n_out=d_ff, epi="swiglu", out_dtype=BF16)
    return matmul(ap, as_, (w_down,), prefix=prefix, n_out=xp.shape[1], epi="residual", coef=0.5,
                  out_dtype=F32, row_extras=((xp, xs),))


def kernel(x_prompt, x_sample, cache_k, cache_v, state_C, state_n, state_m, page_table, norm_g, w_ffn_gate, w_ffn_up, w_ffn_down, sb_w_in, sb_q_norm, sb_k_norm, sb_logit_bias, sb_w_out, ml_w_in, ml_b_gates, ml_h_norm, ml_w_out):
    bp, t_len, d = x_prompt.shape
    bs, ts, _ = x_sample.shape
    depth = norm_g.shape[0]
    sb_heads = sb_logit_bias.shape[1]
    sb_hd = d // sb_heads
    assert sb_hd == V7X_LANES
    ml_heads, ml_dv, ml_dk = state_C.shape[2:]
    tn = V7X_MXU_DIM

    xp = x_prompt.reshape(bp * t_len, d)
    xs = x_sample.reshape(bs * ts, d)
    w_down = w_ffn_down
    kp_rows, vp_rows, ks_rows, vs_rows = [], [], [], []
    c_p, n_p, m_p, c_s, n_s, m_s = [], [], [], [], [], []

    for l in range(depth):
        xp, xs = _ffn_half(xp, xs, norm_g[l, 0], w_ffn_gate, w_ffn_up, w_down, (l, 0))
        hp, hs = rmsnorm(xp, norm_g[l, 1]), rmsnorm(xs, norm_g[l, 1])
        if l % DEPTH_MIXERS == 0:
            a = l // DEPTH_MIXERS
            tile = lambda g: jnp.tile(g, d // sb_hd).reshape(1, d)
            qp, qs = matmul(hp, hs, (sb_w_in,), prefix=(a,), n_out=d, col_off=0, epi="headnorm",
                            coef=LOG2E / math.sqrt(sb_hd), out_dtype=BF16, col_extras=(tile(sb_q_norm[a]),))
            kp, ks = matmul(hp, hs, (sb_w_in,), prefix=(a,), n_out=d, col_off=d // tn, epi="headnorm",
                            out_dtype=F32, col_extras=(tile(sb_k_norm[a]),))
            vp, vs = matmul(hp, hs, (sb_w_in,), prefix=(a,), n_out=d, col_off=2 * d // tn, epi="cast",
                            out_dtype=F32)
            att_p = sb_attend_prompt(qp, kp, vp, sb_logit_bias[a], batch=bp, t_len=t_len)
            att_s = sb_attend_sample(qs.astype(F32), ks, vs, cache_k, cache_v, page_table,
                                     sb_logit_bias[a], layer=a, ts=ts)
            xp, xs = matmul(att_p, att_s.astype(BF16), (sb_w_out,), prefix=(a,), n_out=d, epi="residual",
                            out_dtype=F32, row_extras=((xp, xs),))
            kp_rows.append(kp.reshape(bp, t_len, sb_heads, sb_hd))
            vp_rows.append(vp.reshape(bp, t_len, sb_heads, sb_hd))
            ks_rows.append(ks.reshape(bs, ts, sb_heads, sb_hd))
            vs_rows.append(vs.reshape(bs, ts, sb_heads, sb_hd))
        else:
            jl = l // DEPTH_MIXERS
            s_qk = ml_heads * ml_dk
            proj = functools.partial(matmul, hp, hs, (ml_w_in,), prefix=(jl,))
            qp, qs = proj(n_out=s_qk, col_off=0, epi="cast", out_dtype=BF16)
            kp, ks = proj(n_out=s_qk, col_off=s_qk // tn, epi="cast", coef=ml_dk ** -0.5, out_dtype=BF16)
            vp, vs = proj(n_out=d, col_off=2 * s_qk // tn, epi="cast", out_dtype=BF16)
            op, os_ = proj(n_out=d, col_off=(2 * s_qk + d) // tn, epi="cast", out_dtype=F32)
            n_gate = 2 * ml_heads
            w_g = jnp.pad(ml_w_in[jl:jl + 1, :, 2 * s_qk + 2 * d:], ((0, 0), (0, 0), (0, V7X_LANES - n_gate)))
            b_g = jnp.pad(ml_b_gates[jl].reshape(1, n_gate), ((0, 0), (0, V7X_LANES - n_gate)))
            gp, gs = matmul(hp, hs, (w_g,), prefix=(0,), n_out=V7X_LANES, epi="bias", out_dtype=F32,
                            col_extras=(b_g,), tn=V7X_LANES)

            zeros = lambda *shape: jnp.zeros(shape, F32)
            yp, c, n, m = mlstm(qp, kp, vp, op, gp, ml_h_norm[jl], zeros(bp, ml_heads, ml_dv, ml_dk),
                                zeros(bp, ml_heads, ml_dk), zeros(bp, ml_heads),
                                batch=bp, t_len=t_len, y_dtype=BF16)
            c_p.append(c); n_p.append(n.reshape(bp, ml_heads, ml_dk)); m_p.append(m.reshape(bp, ml_heads))
            f32 = lambda a_: a_.astype(F32)
            ys, c, n, m = mlstm(f32(qs), f32(ks), f32(vs), os_, gs, ml_h_norm[jl], state_C[jl], state_n[jl],
                                state_m[jl], batch=bs, t_len=ts, y_dtype=F32)
            c_s.append(c); n_s.append(n.reshape(bs, ml_heads, ml_dk)); m_s.append(m.reshape(bs, ml_heads))
            xp, xs = matmul(yp, ys.astype(BF16), (ml_w_out,), prefix=(jl,), n_out=d, epi="residual",
                            out_dtype=F32, row_extras=((xp, xs),))
        xp, xs = _ffn_half(xp, xs, norm_g[l, 2], w_ffn_gate, w_ffn_up, w_down, (l, 1))

    stack = lambda rows: rows[0][None] if len(rows) == 1 else jnp.stack(rows)
    return (xp.reshape(bp, t_len, d), xs.reshape(bs, ts, d),
            stack(kp_rows), stack(vp_rows), stack(c_p), stack(n_p), stack(m_p),
            stack(ks_rows), stack(vs_rows), stack(c_s), stack(n_s), stack(m_s))
```

```python
import functools
import math

import jax
import jax.numpy as jnp
from jax import lax
from jax.experimental import pallas as pl
from jax.experimental.pallas import tpu as pltpu

F32 = jnp.float32
BF16 = jnp.bfloat16

NORM_EPS = 1e-6
ML_CHUNK = 128
DEPTH_MIXERS = 2

V7X_VMEM_BYTES = 64 * 1024 * 1024
V7X_LANES = 128
V7X_MXU_DIM = 256
VMEM_LIMIT_BYTES = V7X_VMEM_BYTES - 6 * 1024 * 1024

LOG2E = math.log2(math.e)
MASK_NEG = -1e30
SB_KEY_BLOCK = 128
SB_ROW_CHUNK = 256


def _params(n_axes):
    return pltpu.CompilerParams(dimension_semantics=("arbitrary",) * n_axes,
                                vmem_limit_bytes=VMEM_LIMIT_BYTES)


def _largest_tile(m, cap):
    t = min(m, cap)
    while m % t:
        t -= 8
    return t


def _log_sigmoid(x):
    return jnp.minimum(x, 0.0) - jnp.log(1.0 + jnp.exp(-jnp.abs(x)))


def _log2_sigmoid(x2):
    neg_abs = lax.bitcast_convert_type(lax.bitcast_convert_type(x2, jnp.uint32) | jnp.uint32(1 << 31), F32)
    return jnp.minimum(x2, 0.0) - jnp.log2(1.0 + jnp.exp2(neg_abs))


def _sigmoid(x):
    return 1.0 / (1.0 + jnp.exp(-x))


def _split_bf16(x):
    hi = x.astype(BF16)
    lo = (x - hi.astype(F32)).astype(BF16)
    return hi, lo


def _rmsnorm_body(x_ref, g_ref, o_ref):
    x = x_ref[...]
    ms = jnp.mean(x * x, axis=-1, keepdims=True)
    o_ref[...] = (x * lax.rsqrt(ms + NORM_EPS) * g_ref[...]).astype(o_ref.dtype)


def rmsnorm(x, g):
    m, d = x.shape
    tm = _largest_tile(m, 256)
    return pl.pallas_call(
        _rmsnorm_body,
        grid=(m // tm,),
        in_specs=[pl.BlockSpec((tm, d), lambda i: (i, 0)),
                  pl.BlockSpec((1, d), lambda i: (0, 0))],
        out_specs=pl.BlockSpec((tm, d), lambda i: (i, 0)),
        out_shape=jax.ShapeDtypeStruct((m, d), BF16),
        compiler_params=_params(1),
        name="rmsnorm",
    )(x, g.reshape(1, d))


def _epilogue(accs, extras, *, epi, coef, out_dtype):
    if epi == "swiglu":
        g, u = accs
        return (g * _sigmoid(g) * u).astype(out_dtype)
    (acc,) = accs
    if epi == "cast":
        return (acc * coef).astype(out_dtype) if coef != 1.0 else acc.astype(out_dtype)
    if epi == "bias":
        (b,) = extras
        return (acc + b).astype(out_dtype)
    if epi == "headnorm":
        (gain,) = extras
        outs = []
        for h in range(acc.shape[1] // V7X_LANES):
            a = acc[:, h * V7X_LANES:(h + 1) * V7X_LANES]
            ms = jnp.mean(a * a, axis=-1, keepdims=True)
            y = a * lax.rsqrt(ms + NORM_EPS) * gain[:, h * V7X_LANES:(h + 1) * V7X_LANES]
            outs.append(y * coef if coef != 1.0 else y)
        return jnp.concatenate(outs, axis=1).astype(out_dtype)
    if epi == "residual":
        (x,) = extras
        return (x + coef * acc).astype(out_dtype)
    raise ValueError(epi)


def _mm_body(*refs, n_w, n_row_extra, n_col_extra, epi, coef):
    ap_ref, as_ref = refs[0], refs[1]
    w_refs = refs[2:2 + n_w]
    pos = 2 + n_w
    xp_refs = refs[pos:pos + n_row_extra]
    xs_refs = refs[pos + n_row_extra:pos + 2 * n_row_extra]
    pos += 2 * n_row_extra
    c_refs = refs[pos:pos + n_col_extra]
    op_ref, os_ref = refs[pos + n_col_extra:]

    wb = [w[...].astype(BF16) for w in w_refs]
    col = [c[...] for c in c_refs]

    def run(a_ref, x_refs, o_ref):
        a = a_ref[...]
        accs = [jnp.dot(a, w, preferred_element_type=F32) for w in wb]
        extras = [x[...] for x in x_refs] + col
        o_ref[...] = _epilogue(accs, extras, epi=epi, coef=coef, out_dtype=o_ref.dtype)

    run(ap_ref, xp_refs, op_ref)

    @pl.when(pl.program_id(0) == 0)
    def _():
        run(as_ref, xs_refs, os_ref)


def _mm_tiles(mp, k, n_out, n_w, tn):
    if tn is not None:
        return _largest_tile(mp, 2048), tn, True
    if n_w == 1 and k <= 4096 and n_out % (2 * V7X_MXU_DIM) == 0:
        return _largest_tile(mp, 1024), 2 * V7X_MXU_DIM, False
    return _largest_tile(mp, 2048 if k <= 4096 else 1024), V7X_MXU_DIM, True


def matmul(ap, as_, ws, *, prefix, n_out, col_off=0, epi, coef=1.0, out_dtype,
           row_extras=(), col_extras=(), tn=None):
    mp, k = ap.shape
    ms = as_.shape[0]
    tm, tn, single_a = _mm_tiles(mp, k, n_out, len(ws), tn)
    ni, nj = mp // tm, n_out // tn
    assert n_out % tn == 0 and mp % tm == 0 and col_off % tn == 0
    npre = len(prefix)

    def w_map(i, j):
        return (*prefix, 0, j + col_off // tn)

    def s_col(i, j):
        return (0, jnp.where(i == 0, j, nj - 1))

    in_specs = [pl.BlockSpec((tm, k), lambda i, j: (i, 0), pipeline_mode=pl.Buffered(1 if single_a else 2)),
                pl.BlockSpec((ms, k), lambda i, j: (0, 0))]
    in_specs += [pl.BlockSpec((None,) * npre + (k, tn), w_map) for _ in ws]
    in_specs += [pl.BlockSpec((tm, tn), lambda i, j: (i, j)) for _ in row_extras]
    in_specs += [pl.BlockSpec((ms, tn), s_col) for _ in row_extras]
    in_specs += [pl.BlockSpec((1, tn), lambda i, j: (0, j)) for _ in col_extras]
    args = [ap, as_, *ws, *[e[0] for e in row_extras], *[e[1] for e in row_extras], *col_extras]

    body = functools.partial(_mm_body, n_w=len(ws), n_row_extra=len(row_extras),
                             n_col_extra=len(col_extras), epi=epi, coef=coef)
    return pl.pallas_call(
        body,
        grid=(ni, nj),
        in_specs=in_specs,
        out_specs=[pl.BlockSpec((tm, tn), lambda i, j: (i, j)),
                   pl.BlockSpec((ms, tn), s_col)],
        out_shape=[jax.ShapeDtypeStruct((mp, n_out), out_dtype),
                   jax.ShapeDtypeStruct((ms, n_out), out_dtype)],
        compiler_params=_params(2),
        name="mm_" + epi,
    )(*args)


SB_UNROLL = 8
SB_LAG = 1


def _sb_schedule(t_len):
    kb_sz, rc = SB_KEY_BLOCK, SB_ROW_CHUNK
    r_per = rc // kb_sz
    items = [(kb, c) for kb in reversed(range(t_len // kb_sz)) for c in range(kb // r_per, t_len // rc)]
    n_trip = -(-(len(items) + 5 * SB_LAG) // SB_UNROLL) * SB_UNROLL
    spare = (0, t_len, r_per)

    def entry(i):
        if not 0 <= i < len(items):
            return spare
        kb, c = items[i]
        return kb, c * rc, (kb - c * r_per) if c == kb // r_per else r_per

    cols = [[entry(t - lag * SB_LAG)[f] for t in range(n_trip)] for lag, f in
            ((0, 0), (0, 1), (1, 2), (3, 1), (4, 0), (5, 1))]
    cols[1] = [min(r, t_len - rc) for r in cols[1]]
    return n_trip, jnp.asarray(cols, jnp.int32)


def _sb_prompt_body(tbl_ref, q_ref, k_ref, v_ref, bias_ref, o_ref,
                    ktd, vbd, acc, carry, bvar, *rings, t_len, n_trip):
    z_raw, s_hl, cs_raw, s_w, o_raw = (rings[2 * i:2 * i + 2] for i in range(5))
    s_ls = rings[10:10 + SB_UNROLL]
    kb_sz, rc = SB_KEY_BLOCK, SB_ROW_CHUNK
    nkb, r_per = t_len // kb_sz, rc // kb_sz
    hd = V7X_LANES

    @pl.when((pl.program_id(0) == 0) & (pl.program_id(1) == 0))
    def _():
        ktd[...] = jnp.zeros_like(ktd)
        vbd[...] = jnp.zeros_like(vbd)

    for kb in range(nkb):
        rows = slice(kb * kb_sz, (kb + 1) * kb_sz)
        for e in range(2):
            cols = slice(e * hd, (e + 1) * hd)
            ktd[kb, e * hd:(e + 1) * hd, e * kb_sz:(e + 1) * kb_sz] = k_ref[rows, cols].T.astype(BF16)
            vbd[kb, e * kb_sz:(e + 1) * kb_sz, cols] = v_ref[rows, cols].astype(BF16)

    for buf in (acc, carry, *rings):
        buf[...] = jnp.zeros_like(buf)

    bias = bias_ref[...]
    ri = lax.broadcasted_iota(jnp.int32, (rc, 2 * kb_sz), 0)
    si = lax.broadcasted_iota(jnp.int32, (rc, 2 * kb_sz), 1) & (kb_sz - 1)
    for d in range(r_per):
        bvar[d] = jnp.where(d * kb_sz + si < ri, bias, MASK_NEG)
    bvar[r_per] = jnp.broadcast_to(bias, (rc, 2 * kb_sz))

    jr = lax.broadcasted_iota(jnp.int32, (4 * kb_sz, 4 * kb_sz), 0) & (2 * kb_sz - 1)
    jc = lax.broadcasted_iota(jnp.int32, (4 * kb_sz, 4 * kb_sz), 1)
    same_head = (jr // kb_sz) == ((jc // kb_sz) & 1)
    later = (jc >= 2 * kb_sz) | (jr > jc)
    tri_ones = jnp.where(same_head & later, 1.0, 0.0).astype(BF16)

    def trip(t, phase):
        cur, prev = phase % 2, 1 - phase % 2
        rows = lambda r: pl.ds(pl.multiple_of(tbl_ref[r, t], rc), rc)
        z_raw[cur][...] = jnp.dot(q_ref[rows(1), :], ktd[tbl_ref[0, t]], preferred_element_type=F32)
        z = z_raw[prev][...] + bvar[tbl_ref[2, t]]
        ls = _log2_sigmoid(z)
        hi, lo = _split_bf16(ls - z)
        s_ls[phase][...] = ls
        s_hl[cur][...] = jnp.concatenate([hi, lo], axis=1)
        cs_raw[cur][...] = jnp.dot(s_hl[prev][...], tri_ones, preferred_element_type=F32)
        rows_e2 = rows(3)
        cs = cs_raw[prev][...]
        s_w[cur][...] = jnp.exp2(s_ls[(phase - 2) % SB_UNROLL][...] + cs[:, :2 * kb_sz]
                                 + carry[rows_e2, :]).astype(BF16)
        carry[rows_e2, :] += cs[:, 2 * kb_sz:]
        o_raw[cur][...] = jnp.dot(s_w[prev][...], vbd[tbl_ref[4, t]], preferred_element_type=F32)
        acc[rows(5), :] += o_raw[prev][...]

    def trips(u, _):
        for phase in range(SB_UNROLL):
            trip(SB_UNROLL * u + phase, phase)
        return 0

    lax.fori_loop(0, n_trip // SB_UNROLL, trips, 0)
    o_ref[...] = acc[0:t_len, :].astype(o_ref.dtype)


def sb_attend_prompt(q, k, v, bias, *, batch, t_len):
    m, d = q.shape
    w2 = 2 * V7X_LANES
    n_pair = d // w2
    nkb = t_len // SB_KEY_BLOCK
    rc = SB_ROW_CHUNK
    assert t_len % rc == 0
    n_trip, tables = _sb_schedule(t_len)
    spec = pl.BlockSpec((t_len, w2), lambda b, p, tbl: (b, p))
    grid_spec = pltpu.PrefetchScalarGridSpec(
        num_scalar_prefetch=1,
        grid=(batch, n_pair),
        in_specs=[spec, spec, spec, pl.BlockSpec((1, w2), lambda b, p, tbl: (0, p))],
        out_specs=spec,
        scratch_shapes=[pltpu.VMEM((nkb, w2, w2), BF16),
                        pltpu.VMEM((nkb, w2, w2), BF16),
                        pltpu.VMEM((t_len + rc, w2), F32),
                        pltpu.VMEM((t_len + rc, w2), F32),
                        pltpu.VMEM((rc // SB_KEY_BLOCK + 1, rc, w2), F32),
                        *[pltpu.VMEM((rc, w2), F32)] * 2,
                        *[pltpu.VMEM((rc, 2 * w2), BF16)] * 2,
                        *[pltpu.VMEM((rc, 2 * w2), F32)] * 2,
                        *[pltpu.VMEM((rc, w2), BF16)] * 2,
                        *[pltpu.VMEM((rc, w2), F32)] * 2,
                        *[pltpu.VMEM((rc, w2), F32)] * SB_UNROLL],
    )
    return pl.pallas_call(
        functools.partial(_sb_prompt_body, t_len=t_len, n_trip=n_trip),
        grid_spec=grid_spec,
        out_shape=jax.ShapeDtypeStruct((m, d), BF16),
        compiler_params=_params(2),
        name="sb_attend_prompt",
    )(tables, q, k, v, jnp.repeat(bias * LOG2E, V7X_LANES).reshape(1, d))


SB_HEAD_PITCH = SB_KEY_BLOCK + 8


def _sb_sample_body(pt_ref, q_ref, knew_ref, vnew_ref, *refs, n_steps, per_step, ts):
    kpg_refs, vpg_refs = refs[:per_step], refs[per_step:2 * per_step]
    brow_ref, o_ref, qbd, acc, carry = refs[2 * per_step:2 * per_step + 5]
    hm_bufs = refs[2 * per_step + 5:]
    j = pl.program_id(1)
    hd, pg, pitch = V7X_LANES, SB_KEY_BLOCK, SB_HEAD_PITCH
    n_tile = qbd.shape[0]
    rows_per = 2 * ts

    jr = lax.broadcasted_iota(jnp.int32, (2 * pg, 2 * pg), 0) & (pg - 1)
    jc = lax.broadcasted_iota(jnp.int32, (2 * pg, 2 * pg), 1)
    tri_ones = jnp.where((jc >= pg) | (jr > jc), 1.0, 0.0).astype(BF16)

    def regroup(src_ref, dst):
        for p in range(pg):
            for g in range(src_ref.shape[1] // 8):
                dst[pl.ds(8 * g * pitch + p, 8, stride=pitch), :] = src_ref[p, 8 * g:8 * g + 8, :]

    def head_pair(buf, n):
        return jnp.concatenate([buf[h * pitch:h * pitch + pg, :].astype(BF16) for h in (2 * n, 2 * n + 1)],
                               axis=1)

    def process(k_ref, v_ref, khm, vhm, masked):
        regroup(k_ref, khm)
        regroup(v_ref, vhm)
        zs = []
        for n in range(n_tile):
            zs.append(lax.dot_general(qbd[n].astype(BF16), head_pair(khm, n), (((1,), (1,)), ((), ())),
                                      preferred_element_type=F32))
        z = jnp.concatenate(zs, axis=0) + brow_ref[...]
        if masked:
            qi = lax.broadcasted_iota(jnp.int32, z.shape, 0) % ts
            si = lax.broadcasted_iota(jnp.int32, z.shape, 1)
            z = jnp.where(si < qi, z, MASK_NEG)
        ls = _log2_sigmoid(z)
        hi, lo = _split_bf16(ls - z)
        cs = jnp.dot(jnp.concatenate([hi, lo], axis=1), tri_ones,
                     preferred_element_type=F32)
        w = jnp.exp2(ls + cs[:, :pg] + carry[...]).astype(BF16)
        for n in range(n_tile):
            acc[n] += jnp.dot(w[n * rows_per:(n + 1) * rows_per, :], head_pair(vhm, n),
                              preferred_element_type=F32)
        carry[...] += cs[:, pg:]

    @pl.when(j == 0)
    def _():
        qbd[...] = jnp.zeros_like(qbd)
        for n in range(n_tile):
            for e in range(2):
                qbd[n, e * ts:(e + 1) * ts, e * hd:(e + 1) * hd] = \
                    q_ref[:, (2 * n + e) * hd:(2 * n + e + 1) * hd]
        acc[...] = jnp.zeros_like(acc)
        carry[...] = jnp.zeros_like(carry)
        process(knew_ref, vnew_ref, hm_bufs[0], hm_bufs[1], True)

    for i, (k_ref, v_ref) in enumerate(zip(kpg_refs, vpg_refs)):
        process(k_ref, v_ref, hm_bufs[2 * i], hm_bufs[2 * i + 1], False)

    @pl.when(j == n_steps - 1)
    def _():
        for n in range(n_tile):
            for e in range(2):
                o_ref[:, (2 * n + e) * hd:(2 * n + e + 1) * hd] = \
                    acc[n, e * ts:(e + 1) * ts, e * hd:(e + 1) * hd]


def sb_attend_sample(q, k_new, v_new, cache_k, cache_v, page_table, bias, *, layer, ts):
    m, d = q.shape
    bs = m // ts
    n_pages = page_table.shape[1]
    n_layers, n_phys, pg = cache_k.shape[:3]
    assert pg == SB_KEY_BLOCK and ts % 8 == 0
    heads, hd = cache_k.shape[3:]
    assert hd == V7X_LANES and heads * hd == d and heads % 8 == 0
    n_tile = heads // 2
    rows = heads * ts
    pad = lambda a: jnp.pad(a.reshape(bs, ts, heads, hd), ((0, 0), (0, pg - ts), (0, 0), (0, 0)))
    brow = jnp.broadcast_to(jnp.repeat(bias * LOG2E, ts)[:, None], (rows, pg))

    per_step = 2 if n_pages % 2 == 0 else 1
    n_steps = n_pages // per_step
    page_spec = lambda i: pl.BlockSpec(
        (None, None, pg, heads, hd), lambda b, j, pt: (layer, pt[b, n_pages - 1 - (j * per_step + i)], 0, 0, 0))
    new_map = lambda b, j, pt: (b, 0, 0, 0)
    grid_spec = pltpu.PrefetchScalarGridSpec(
        num_scalar_prefetch=1,
        grid=(bs, n_steps),
        in_specs=[pl.BlockSpec((ts, d), lambda b, j, pt: (b, 0)),
                  pl.BlockSpec((None, pg, heads, hd), new_map),
                  pl.BlockSpec((None, pg, heads, hd), new_map),
                  *[page_spec(i) for i in range(per_step)],
                  *[page_spec(i) for i in range(per_step)],
                  pl.BlockSpec((rows, pg), lambda b, j, pt: (0, 0))],
        out_specs=pl.BlockSpec((ts, d), lambda b, j, pt: (b, 0)),
        scratch_shapes=[pltpu.VMEM((n_tile, 2 * ts, 2 * hd), F32),
                        pltpu.VMEM((n_tile, 2 * ts, 2 * hd), F32),
                        pltpu.VMEM((rows, pg), F32),
                        *[pltpu.VMEM((heads * SB_HEAD_PITCH, hd), F32)] * (2 * per_step)],
    )
    return pl.pallas_call(
        functools.partial(_sb_sample_body, n_steps=n_steps, per_step=per_step, ts=ts),
        grid_spec=grid_spec,
        out_shape=jax.ShapeDtypeStruct((m, d), F32),
        compiler_params=_params(2),
        name="sb_attend_sample",
    )(page_table, q, pad(k_new), pad(v_new), *[cache_k] * per_step, *[cache_v] * per_step, brow)


def _mxu_operand(x):
    y = x.astype(BF16)
    return y if x.shape[0] >= 16 else y.astype(F32)


def _mlstm_body(q_ref, k_ref, v_ref, o_ref, ir_ref, ic_ref, fr_ref, fc_ref, hg_ref,
                c0_ref, n0_ref, m0_ref, y_ref, c_ref, n_ref, m_ref, ct, ns, ms, *, chunk, n_chunks):
    c = pl.program_id(2)
    ln = chunk

    @pl.when(c == 0)
    def _():
        ct[...] = c0_ref[...].T
        ns[...] = n0_ref[...]
        ms[...] = m0_ref[...]

    q = _mxu_operand(q_ref[...])
    k = _mxu_operand(k_ref[...])
    v = _mxu_operand(v_ref[...])
    lf_r = _log_sigmoid(fr_ref[...])
    lf_c = _log_sigmoid(fc_ref[...])
    ti = lax.broadcasted_iota(jnp.int32, (ln, ln), 0)
    si = lax.broadcasted_iota(jnp.int32, (ln, ln), 1)
    causal = si <= ti
    b_col = jnp.sum(jnp.where(causal, lf_r, 0.0), axis=1, keepdims=True)
    b_row = jnp.sum(jnp.where(ti <= si, lf_c, 0.0), axis=0, keepdims=True)
    dmat = jnp.where(causal, b_col - b_row + ir_ref[...], -jnp.inf)
    m_prev = ms[...]
    g = b_col + m_prev
    m_t = jnp.maximum(g, jnp.max(dmat, axis=1, keepdims=True))
    s = lax.dot_general(q, k, (((1,), (1,)), ((), ())), preferred_element_type=F32)
    w = jnp.exp(dmat - m_t) * s
    w_carry = jnp.exp(g - m_t)
    num = (w_carry * jnp.dot(q, _mxu_operand(ct[...]), preferred_element_type=F32)
           + jnp.dot(_mxu_operand(w), v, preferred_element_type=F32))
    qn = jnp.sum(q.astype(F32) * ns[...], axis=1, keepdims=True)
    den = w_carry * qn + jnp.sum(w, axis=1, keepdims=True)
    h = num / jnp.maximum(jnp.abs(den), jnp.exp(-m_t))

    hms = jnp.mean(h * h, axis=1, keepdims=True)
    hn = h * lax.rsqrt(hms + NORM_EPS) * hg_ref[...]
    y_ref[...] = (_sigmoid(o_ref[...]) * hn).astype(y_ref.dtype)

    m_new = m_t[ln - 1:ln, :]
    b_last = b_col[ln - 1:ln, :]
    decay = jnp.exp(b_last - b_col + ic_ref[...] - m_new)
    c_scale = jnp.exp(b_last + m_prev - m_new)
    kd = decay * k.astype(F32)
    ct[...] = c_scale * ct[...] + lax.dot_general(
        _mxu_operand(kd), v, (((0,), (0,)), ((), ())), preferred_element_type=F32)
    ns[...] = c_scale * ns[...] + jnp.sum(kd, axis=0, keepdims=True)
    ms[...] = m_new

    @pl.when(c == n_chunks - 1)
    def _():
        c_ref[...] = ct[...].T
        n_ref[...] = ns[...]
        m_ref[...] = ms[...]


def mlstm(q, k, v, o, gates, h_gain, c0, n0, m0, *, batch, t_len, y_dtype):
    heads, dv, dk = c0.shape[1:]
    chunk = ML_CHUNK if t_len % ML_CHUNK == 0 else t_len
    nc = t_len // chunk
    gt = gates[:, :2 * heads].reshape(batch, nc, chunk, 2, heads)
    gt = jnp.transpose(gt, (3, 0, 4, 1, 2))
    i_row, f_row = gt[0][..., None, :], gt[1][..., None, :]
    i_col, f_col = gt[0][..., :, None], gt[1][..., :, None]

    tok = lambda width: pl.BlockSpec((chunk, width), lambda b, h, c: (b * nc + c, h))
    row = pl.BlockSpec((None, None, None, 1, chunk), lambda b, h, c: (b, h, c, 0, 0))
    col = pl.BlockSpec((None, None, None, chunk, 1), lambda b, h, c: (b, h, c, 0, 0))
    st = lambda r, w: pl.BlockSpec((None, None, r, w), lambda b, h, c: (b, h, 0, 0))
    return pl.pallas_call(
        functools.partial(_mlstm_body, chunk=chunk, n_chunks=nc),
        grid=(batch, heads, nc),
        in_specs=[tok(dk), tok(dk), tok(dv), tok(dv), row, col, row, col,
                  pl.BlockSpec((1, dv), lambda b, h, c: (0, h)),
                  st(dv, dk), st(1, dk), st(1, 1)],
        out_specs=[tok(dv), st(dv, dk), st(1, dk), st(1, 1)],
        out_shape=[jax.ShapeDtypeStruct((batch * t_len, heads * dv), y_dtype),
                   jax.ShapeDtypeStruct((batch, heads, dv, dk), F32),
                   jax.ShapeDtypeStruct((batch, heads, 1, dk), F32),
                   jax.ShapeDtypeStruct((batch, heads, 1, 1), F32)],
        scratch_shapes=[pltpu.VMEM((dk, dv), F32), pltpu.VMEM((1, dk), F32), pltpu.VMEM((1, 1), F32)],
        compiler_params=_params(3),
        name="mlstm",
    )(q, k, v, o, i_row, i_col, f_row, f_col, h_gain.reshape(1, heads * dv),
      c0, n0.reshape(batch, heads, 1, dk), m0.reshape(batch, heads, 1, 1))


def _ffn_half(xp, xs, gain, w_gate, w_up, w_down, prefix):
    hp, hs = rmsnorm(xp, gain), rmsnorm(xs, gain)
    d_ff = w_gate.shape[-1]
    ap, as_ = matmul(hp, hs, (w_gate, w_up), prefix=prefix, n_out=d_ff, epi="swiglu", out_dtype=BF16)
    return matmul(ap, as_, (w_down,), prefix=prefix, n_out=xp.shape[1], epi="residual", coef=0.5,
                  out_dtype=F32, row_extras=((xp, xs),))


def kernel(x_prompt, x_sample, cache_k, cache_v, state_C, state_n, state_m, page_table, norm_g, w_ffn_gate, w_ffn_up, w_ffn_down, sb_w_in, sb_q_norm, sb_k_norm, sb_logit_bias, sb_w_out, ml_w_in, ml_b_gates, ml_h_norm, ml_w_out):
    bp, t_len, d = x_prompt.shape
    bs, ts, _ = x_sample.shape
    depth = norm_g.shape[0]
    sb_heads = sb_logit_bias.shape[1]
    sb_hd = d // sb_heads
    assert sb_hd == V7X_LANES
    ml_heads, ml_dv, ml_dk = state_C.shape[2:]

    xp = x_prompt.reshape(bp * t_len, d)
    xs = x_sample.reshape(bs * ts, d)
    w_down = w_ffn_down
    kp_rows, vp_rows, ks_rows, vs_rows = [], [], [], []
    c_p, n_p, m_p, c_s, n_s, m_s = [], [], [], [], [], []

    for l in range(depth):
        xp, xs = _ffn_half(xp, xs, norm_g[l, 0], w_ffn_gate, w_ffn_up, w_down, (l, 0))
        hp, hs = rmsnorm(xp, norm_g[l, 1]), rmsnorm(xs, norm_g[l, 1])
        if l % DEPTH_MIXERS == 0:
            a = l // DEPTH_MIXERS
            tile = lambda g: jnp.tile(g, d // sb_hd).reshape(1, d)
            qp, qs = matmul(hp, hs, (sb_w_in,), prefix=(a,), n_out=d, col_off=0, epi="headnorm",
                            coef=LOG2E / math.sqrt(sb_hd), out_dtype=BF16, col_extras=(tile(sb_q_norm[a]),))
            kp, ks = matmul(hp, hs, (sb_w_in,), prefix=(a,), n_out=d, col_off=d, epi="headnorm",
                            out_dtype=F32, col_extras=(tile(sb_k_norm[a]),))
            vp, vs = matmul(hp, hs, (sb_w_in,), prefix=(a,), n_out=d, col_off=2 * d, epi="cast",
                            out_dtype=F32)
            att_p = sb_attend_prompt(qp, kp, vp, sb_logit_bias[a], batch=bp, t_len=t_len)
            att_s = sb_attend_sample(qs.astype(F32), ks, vs, cache_k, cache_v, page_table,
                                     sb_logit_bias[a], layer=a, ts=ts)
            xp, xs = matmul(att_p, att_s.astype(BF16), (sb_w_out,), prefix=(a,), n_out=d, epi="residual",
                            out_dtype=F32, row_extras=((xp, xs),))
            kp_rows.append(kp.reshape(bp, t_len, sb_heads, sb_hd))
            vp_rows.append(vp.reshape(bp, t_len, sb_heads, sb_hd))
            ks_rows.append(ks.reshape(bs, ts, sb_heads, sb_hd))
            vs_rows.append(vs.reshape(bs, ts, sb_heads, sb_hd))
        else:
            jl = l // DEPTH_MIXERS
            s_qk = ml_heads * ml_dk
            proj = functools.partial(matmul, hp, hs, (ml_w_in,), prefix=(jl,))
            qp, qs = proj(n_out=s_qk, col_off=0, epi="cast", out_dtype=BF16)
            kp, ks = proj(n_out=s_qk, col_off=s_qk, epi="cast", coef=ml_dk ** -0.5, out_dtype=BF16)
            vp, vs = proj(n_out=d, col_off=2 * s_qk, epi="cast", out_dtype=BF16)
            op, os_ = proj(n_out=d, col_off=2 * s_qk + d, epi="cast", out_dtype=F32)
            n_gate = 2 * ml_heads
            w_g = jnp.pad(ml_w_in[jl:jl + 1, :, 2 * s_qk + 2 * d:], ((0, 0), (0, 0), (0, V7X_LANES - n_gate)))
            b_g = jnp.pad(ml_b_gates[jl].reshape(1, n_gate), ((0, 0), (0, V7X_LANES - n_gate)))
            gp, gs = matmul(hp, hs, (w_g,), prefix=(0,), n_out=V7X_LANES, epi="bias", out_dtype=F32,
                            col_extras=(b_g,), tn=V7X_LANES)

            zeros = lambda *shape: jnp.zeros(shape, F32)
            yp, c, n, m = mlstm(qp, kp, vp, op, gp, ml_h_norm[jl], zeros(bp, ml_heads, ml_dv, ml_dk),
                                zeros(bp, ml_heads, ml_dk), zeros(bp, ml_heads),
                                batch=bp, t_len=t_len, y_dtype=BF16)
            c_p.append(c); n_p.append(n.reshape(bp, ml_heads, ml_dk)); m_p.append(m.reshape(bp, ml_heads))
            f32 = lambda a_: a_.astype(F32)
            ys, c, n, m = mlstm(f32(qs), f32(ks), f32(vs), os_, gs, ml_h_norm[jl], state_C[jl], state_n[jl],
                                state_m[jl], batch=bs, t_len=ts, y_dtype=F32)
            c_s.append(c); n_s.append(n.reshape(bs, ml_heads, ml_dk)); m_s.append(m.reshape(bs, ml_heads))
            xp, xs = matmul(yp, ys.astype(BF16), (ml_w_out,), prefix=(jl,), n_out=d, epi="residual",
                            out_dtype=F32, row_extras=((xp, xs),))
        xp, xs = _ffn_half(xp, xs, norm_g[l, 2], w_ffn_gate, w_ffn_up, w_down, (l, 1))

    stack = lambda rows: rows[0][None] if len(rows) == 1 else jnp.stack(rows)
    return (xp.reshape(bp, t_len, d), xs.reshape(bs, ts, d),
            stack(kp_rows), stack(vp_rows), stack(c_p), stack(n_p), stack(m_p),
            stack(ks_rows), stack(vs_rows), stack(c_s), stack(n_s), stack(m_s))
```

```python
import functools
import math

import jax
import jax.numpy as jnp
from jax import lax
from jax.experimental import pallas as pl
from jax.experimental.pallas import tpu as pltpu

F32 = jnp.float32
BF16 = jnp.bfloat16

NORM_EPS = 1e-6
ML_CHUNK = 128
DEPTH_MIXERS = 2

V7X_VMEM_BYTES = 64 * 1024 * 1024
V7X_LANES = 128
V7X_MXU_DIM = 256
VMEM_LIMIT_BYTES = V7X_VMEM_BYTES - 6 * 1024 * 1024

LOG2E = math.log2(math.e)
MASK_NEG = -1e30
SB_KEY_BLOCK = 128
SB_ROW_CHUNK = 256


def _params(n_axes):
    return pltpu.CompilerParams(dimension_semantics=("arbitrary",) * n_axes,
                                vmem_limit_bytes=VMEM_LIMIT_BYTES)


def _largest_tile(m, cap):
    t = min(m, cap)
    while m % t:
        t -= 8
    return t


def _log_sigmoid(x):
    return jnp.minimum(x, 0.0) - jnp.log(1.0 + jnp.exp(-jnp.abs(x)))


def _log2_sigmoid(x2):
    neg_abs = lax.bitcast_convert_type(lax.bitcast_convert_type(x2, jnp.uint32) | jnp.uint32(1 << 31), F32)
    return jnp.minimum(x2, 0.0) - jnp.log2(1.0 + jnp.exp2(neg_abs))


def _sigmoid(x):
    return 1.0 / (1.0 + jnp.exp(-x))


def _split_bf16(x):
    hi = x.astype(BF16)
    lo = (x - hi.astype(F32)).astype(BF16)
    return hi, lo


def _rmsnorm_body(x_ref, g_ref, o_ref):
    x = x_ref[...]
    ms = jnp.mean(x * x, axis=-1, keepdims=True)
    o_ref[...] = (x * lax.rsqrt(ms + NORM_EPS) * g_ref[...]).astype(o_ref.dtype)


def rmsnorm(x, g):
    m, d = x.shape
    tm = _largest_tile(m, 256)
    return pl.pallas_call(
        _rmsnorm_body,
        grid=(m // tm,),
        in_specs=[pl.BlockSpec((tm, d), lambda i: (i, 0)),
                  pl.BlockSpec((1, d), lambda i: (0, 0))],
        out_specs=pl.BlockSpec((tm, d), lambda i: (i, 0)),
        out_shape=jax.ShapeDtypeStruct((m, d), BF16),
        compiler_params=_params(1),
        name="rmsnorm",
    )(x, g.reshape(1, d))


def _epilogue(accs, extras, *, epi, coef, out_dtype):
    if epi == "swiglu":
        g, u = accs
        return (g * _sigmoid(g) * u).astype(out_dtype)
    (acc,) = accs
    if epi == "cast":
        return (acc * coef).astype(out_dtype) if coef != 1.0 else acc.astype(out_dtype)
    if epi == "bias":
        (b,) = extras
        return (acc + b).astype(out_dtype)
    if epi == "headnorm":
        (gain,) = extras
        outs = []
        for h in range(acc.shape[1] // V7X_LANES):
            a = acc[:, h * V7X_LANES:(h + 1) * V7X_LANES]
            ms = jnp.mean(a * a, axis=-1, keepdims=True)
            y = a * lax.rsqrt(ms + NORM_EPS) * gain[:, h * V7X_LANES:(h + 1) * V7X_LANES]
            outs.append(y * coef if coef != 1.0 else y)
        return jnp.concatenate(outs, axis=1).astype(out_dtype)
    if epi == "residual":
        (x,) = extras
        return (x + coef * acc).astype(out_dtype)
    raise ValueError(epi)


def _mm_body(*refs, n_w, n_row_extra, n_col_extra, epi, coef, w_is_nk):
    ap_ref, as_ref = refs[0], refs[1]
    contract = (((1,), (1 if w_is_nk else 0,)), ((), ()))
    w_refs = refs[2:2 + n_w]
    pos = 2 + n_w
    xp_refs = refs[pos:pos + n_row_extra]
    xs_refs = refs[pos + n_row_extra:pos + 2 * n_row_extra]
    pos += 2 * n_row_extra
    c_refs = refs[pos:pos + n_col_extra]
    op_ref, os_ref = refs[pos + n_col_extra:]

    wb = [w[...].astype(BF16) for w in w_refs]
    col = [c[...] for c in c_refs]

    def run(a_ref, x_refs, o_ref):
        a = a_ref[...]
        accs = [lax.dot_general(a, w, contract, preferred_element_type=F32) for w in wb]
        extras = [x[...] for x in x_refs] + col
        o_ref[...] = _epilogue(accs, extras, epi=epi, coef=coef, out_dtype=o_ref.dtype)

    run(ap_ref, xp_refs, op_ref)

    @pl.when(pl.program_id(0) == 0)
    def _():
        run(as_ref, xs_refs, os_ref)


def _mm_tiles(mp, k, n_out, n_w, tn):
    if tn is not None:
        return _largest_tile(mp, 2048), tn, True
    if n_w == 1 and k <= 4096 and n_out % (2 * V7X_MXU_DIM) == 0:
        return _largest_tile(mp, 1024), 2 * V7X_MXU_DIM, False
    return _largest_tile(mp, 2048 if k <= 4096 else 1024), V7X_MXU_DIM, True


def matmul(ap, as_, ws, *, prefix, n_out, col_off=0, epi, coef=1.0, out_dtype,
           row_extras=(), col_extras=(), tn=None, w_is_nk=False):
    mp, k = ap.shape
    ms = as_.shape[0]
    tm, tn, single_a = _mm_tiles(mp, k, n_out, len(ws), tn)
    ni, nj = mp // tm, n_out // tn
    assert n_out % tn == 0 and mp % tm == 0 and col_off % tn == 0
    npre = len(prefix)
    w_block = (tn, k) if w_is_nk else (k, tn)

    def w_map(i, j):
        jw = j + col_off // tn
        return (*prefix, jw, 0) if w_is_nk else (*prefix, 0, jw)

    def s_col(i, j):
        return (0, jnp.where(i == 0, j, nj - 1))

    in_specs = [pl.BlockSpec((tm, k), lambda i, j: (i, 0), pipeline_mode=pl.Buffered(1 if single_a else 2)),
                pl.BlockSpec((ms, k), lambda i, j: (0, 0))]
    in_specs += [pl.BlockSpec((None,) * npre + w_block, w_map) for _ in ws]
    in_specs += [pl.BlockSpec((tm, tn), lambda i, j: (i, j)) for _ in row_extras]
    in_specs += [pl.BlockSpec((ms, tn), s_col) for _ in row_extras]
    in_specs += [pl.BlockSpec((1, tn), lambda i, j: (0, j)) for _ in col_extras]
    args = [ap, as_, *ws, *[e[0] for e in row_extras], *[e[1] for e in row_extras], *col_extras]

    body = functools.partial(_mm_body, n_w=len(ws), n_row_extra=len(row_extras),
                             n_col_extra=len(col_extras), epi=epi, coef=coef, w_is_nk=w_is_nk)
    return pl.pallas_call(
        body,
        grid=(ni, nj),
        in_specs=in_specs,
        out_specs=[pl.BlockSpec((tm, tn), lambda i, j: (i, j)),
                   pl.BlockSpec((ms, tn), s_col)],
        out_shape=[jax.ShapeDtypeStruct((mp, n_out), out_dtype),
                   jax.ShapeDtypeStruct((ms, n_out), out_dtype)],
        compiler_params=_params(2),
        name="mm_" + epi,
    )(*args)


SB_UNROLL = 8
SB_LAG = 1


def _sb_schedule(t_len):
    kb_sz, rc = SB_KEY_BLOCK, SB_ROW_CHUNK
    r_per = rc // kb_sz
    items = [(kb, c) for kb in reversed(range(t_len // kb_sz)) for c in range(kb // r_per, t_len // rc)]
    n_trip = -(-(len(items) + 5 * SB_LAG) // SB_UNROLL) * SB_UNROLL
    spare = (0, t_len, r_per)

    def entry(i):
        if not 0 <= i < len(items):
            return spare
        kb, c = items[i]
        return kb, c * rc, (kb - c * r_per) if c == kb // r_per else r_per

    cols = [[entry(t - lag * SB_LAG)[f] for t in range(n_trip)] for lag, f in
            ((0, 0), (0, 1), (1, 2), (3, 1), (4, 0), (5, 1))]
    cols[1] = [min(r, t_len - rc) for r in cols[1]]
    return n_trip, jnp.asarray(cols, jnp.int32)


def _sb_prompt_body(tbl_ref, q_ref, k_ref, v_ref, bias_ref, o_ref,
                    ktd, vbd, acc, carry, bvar, *rings, t_len, n_trip):
    z_raw, s_hl, cs_raw, s_w, o_raw = (rings[2 * i:2 * i + 2] for i in range(5))
    s_ls = rings[10:10 + SB_UNROLL]
    kb_sz, rc = SB_KEY_BLOCK, SB_ROW_CHUNK
    nkb, r_per = t_len // kb_sz, rc // kb_sz
    hd = V7X_LANES

    @pl.when((pl.program_id(0) == 0) & (pl.program_id(1) == 0))
    def _():
        ktd[...] = jnp.zeros_like(ktd)
        vbd[...] = jnp.zeros_like(vbd)

    for kb in range(nkb):
        rows = slice(kb * kb_sz, (kb + 1) * kb_sz)
        for e in range(2):
            cols = slice(e * hd, (e + 1) * hd)
            ktd[kb, e * hd:(e + 1) * hd, e * kb_sz:(e + 1) * kb_sz] = k_ref[rows, cols].T.astype(BF16)
            vbd[kb, e * kb_sz:(e + 1) * kb_sz, cols] = v_ref[rows, cols].astype(BF16)

    for buf in (acc, carry, *rings):
        buf[...] = jnp.zeros_like(buf)

    bias = bias_ref[...]
    ri = lax.broadcasted_iota(jnp.int32, (rc, 2 * kb_sz), 0)
    si = lax.broadcasted_iota(jnp.int32, (rc, 2 * kb_sz), 1) & (kb_sz - 1)
    for d in range(r_per):
        bvar[d] = jnp.where(d * kb_sz + si < ri, bias, MASK_NEG)
    bvar[r_per] = jnp.broadcast_to(bias, (rc, 2 * kb_sz))

    jr = lax.broadcasted_iota(jnp.int32, (4 * kb_sz, 2 * kb_sz), 0) & (2 * kb_sz - 1)
    jc = lax.broadcasted_iota(jnp.int32, (4 * kb_sz, 2 * kb_sz), 1)
    same_head = (jr // kb_sz) == (jc // kb_sz)
    tri = jnp.where(same_head & ((jr > jc) | ((jc & (kb_sz - 1)) == 0)), 1.0, 0.0).astype(BF16)
    first_key = (lax.broadcasted_iota(jnp.int32, (rc, 2 * kb_sz), 1) & (kb_sz - 1)) == 0

    def trip(t, phase):
        cur, prev = phase % 2, 1 - phase % 2
        rows = lambda r: pl.ds(pl.multiple_of(tbl_ref[r, t], rc), rc)
        z_raw[cur][...] = jnp.dot(q_ref[rows(1), :], ktd[tbl_ref[0, t]], preferred_element_type=F32)
        z = z_raw[prev][...] + bvar[tbl_ref[2, t]]
        ls = _log2_sigmoid(z)
        hi, lo = _split_bf16(ls - z)
        s_ls[phase][...] = jnp.where(first_key, z, ls)
        s_hl[cur][...] = jnp.concatenate([hi, lo], axis=1)
        cs_raw[cur][...] = jnp.dot(s_hl[prev][...], tri, preferred_element_type=F32)
        rows_e2 = rows(3)
        cs = cs_raw[prev][...]
        s_w[cur][...] = jnp.exp2(s_ls[(phase - 2) % SB_UNROLL][...] + cs + carry[rows_e2, :]).astype(BF16)
        carry[rows_e2, :] += jnp.concatenate(
            [jnp.broadcast_to(cs[:, e * kb_sz:e * kb_sz + 1], (rc, kb_sz)) for e in range(2)], axis=1)
        o_raw[cur][...] = jnp.dot(s_w[prev][...], vbd[tbl_ref[4, t]], preferred_element_type=F32)
        acc[rows(5), :] += o_raw[prev][...]

    def trips(u, _):
        for phase in range(SB_UNROLL):
            trip(SB_UNROLL * u + phase, phase)
        return 0

    lax.fori_loop(0, n_trip // SB_UNROLL, trips, 0)
    o_ref[...] = acc[0:t_len, :].astype(o_ref.dtype)


def sb_attend_prompt(q, k, v, bias, *, batch, t_len):
    m, d = q.shape
    w2 = 2 * V7X_LANES
    n_pair = d // w2
    nkb = t_len // SB_KEY_BLOCK
    rc = SB_ROW_CHUNK
    assert t_len % rc == 0
    n_trip, tables = _sb_schedule(t_len)
    spec = pl.BlockSpec((t_len, w2), lambda b, p, tbl: (b, p))
    grid_spec = pltpu.PrefetchScalarGridSpec(
        num_scalar_prefetch=1,
        grid=(batch, n_pair),
        in_specs=[spec, spec, spec, pl.BlockSpec((1, w2), lambda b, p, tbl: (0, p))],
        out_specs=spec,
        scratch_shapes=[pltpu.VMEM((nkb, w2, w2), BF16),
                        pltpu.VMEM((nkb, w2, w2), BF16),
                        pltpu.VMEM((t_len + rc, w2), F32),
                        pltpu.VMEM((t_len + rc, w2), F32),
                        pltpu.VMEM((rc // SB_KEY_BLOCK + 1, rc, w2), F32),
                        *[pltpu.VMEM((rc, w2), F32)] * 2,
                        *[pltpu.VMEM((rc, 2 * w2), BF16)] * 2,
                        *[pltpu.VMEM((rc, w2), F32)] * 2,
                        *[pltpu.VMEM((rc, w2), BF16)] * 2,
                        *[pltpu.VMEM((rc, w2), F32)] * 2,
                        *[pltpu.VMEM((rc, w2), F32)] * SB_UNROLL],
    )
    return pl.pallas_call(
        functools.partial(_sb_prompt_body, t_len=t_len, n_trip=n_trip),
        grid_spec=grid_spec,
        out_shape=jax.ShapeDtypeStruct((m, d), BF16),
        compiler_params=_params(2),
        name="sb_attend_prompt",
    )(tables, q, k, v, jnp.repeat(bias * LOG2E, V7X_LANES).reshape(1, d))


SB_HEAD_PITCH = SB_KEY_BLOCK + 8


def _sb_sample_body(pt_ref, q_ref, knew_ref, vnew_ref, *refs, n_steps, per_step, ts):
    kpg_refs, vpg_refs = refs[:per_step], refs[per_step:2 * per_step]
    brow_ref, o_ref, qbd, acc, carry = refs[2 * per_step:2 * per_step + 5]
    hm_bufs = refs[2 * per_step + 5:]
    j = pl.program_id(1)
    hd, pg, pitch = V7X_LANES, SB_KEY_BLOCK, SB_HEAD_PITCH
    n_tile = qbd.shape[0]
    rows_per = 2 * ts

    jr = lax.broadcasted_iota(jnp.int32, (2 * pg, 2 * pg), 0) & (pg - 1)
    jc = lax.broadcasted_iota(jnp.int32, (2 * pg, 2 * pg), 1)
    tri_ones = jnp.where((jc >= pg) | (jr > jc), 1.0, 0.0).astype(BF16)

    def regroup(src_ref, dst):
        for p in range(pg):
            for g in range(src_ref.shape[1] // 8):
                dst[pl.ds(8 * g * pitch + p, 8, stride=pitch), :] = src_ref[p, 8 * g:8 * g + 8, :]

    def head_pair(buf, n):
        return jnp.concatenate([buf[h * pitch:h * pitch + pg, :].astype(BF16) for h in (2 * n, 2 * n + 1)],
                               axis=1)

    def process(k_ref, v_ref, khm, vhm, masked):
        regroup(k_ref, khm)
        regroup(v_ref, vhm)
        zs = []
        for n in range(n_tile):
            zs.append(lax.dot_general(qbd[n].astype(BF16), head_pair(khm, n), (((1,), (1,)), ((), ())),
                                      preferred_element_type=F32))
        z = jnp.concatenate(zs, axis=0) + brow_ref[...]
        if masked:
            qi = lax.broadcasted_iota(jnp.int32, z.shape, 0) % ts
            si = lax.broadcasted_iota(jnp.int32, z.shape, 1)
            z = jnp.where(si < qi, z, MASK_NEG)
        ls = _log2_sigmoid(z)
        hi, lo = _split_bf16(ls - z)
        cs = jnp.dot(jnp.concatenate([hi, lo], axis=1), tri_ones,
                     preferred_element_type=F32)
        w = jnp.exp2(ls + cs[:, :pg] + carry[...]).astype(BF16)
        for n in range(n_tile):
            acc[n] += jnp.dot(w[n * rows_per:(n + 1) * rows_per, :], head_pair(vhm, n),
                              preferred_element_type=F32)
        carry[...] += cs[:, pg:]

    @pl.when(j == 0)
    def _():
        qbd[...] = jnp.zeros_like(qbd)
        for n in range(n_tile):
            for e in range(2):
                qbd[n, e * ts:(e + 1) * ts, e * hd:(e + 1) * hd] = \
                    q_ref[:, (2 * n + e) * hd:(2 * n + e + 1) * hd]
        acc[...] = jnp.zeros_like(acc)
        carry[...] = jnp.zeros_like(carry)
        process(knew_ref, vnew_ref, hm_bufs[0], hm_bufs[1], True)

    for i, (k_ref, v_ref) in enumerate(zip(kpg_refs, vpg_refs)):
        process(k_ref, v_ref, hm_bufs[2 * i], hm_bufs[2 * i + 1], False)

    @pl.when(j == n_steps - 1)
    def _():
        for n in range(n_tile):
            for e in range(2):
                o_ref[:, (2 * n + e) * hd:(2 * n + e + 1) * hd] = \
                    acc[n, e * ts:(e + 1) * ts, e * hd:(e + 1) * hd]


def sb_attend_sample(q, k_new, v_new, cache_k, cache_v, page_table, bias, *, layer, ts):
    m, d = q.shape
    bs = m // ts
    n_pages = page_table.shape[1]
    n_layers, n_phys, pg = cache_k.shape[:3]
    assert pg == SB_KEY_BLOCK and ts % 8 == 0
    heads, hd = cache_k.shape[3:]
    assert hd == V7X_LANES and heads * hd == d and heads % 8 == 0
    n_tile = heads // 2
    rows = heads * ts
    pad = lambda a: jnp.pad(a.reshape(bs, ts, heads, hd), ((0, 0), (0, pg - ts), (0, 0), (0, 0)))
    brow = jnp.broadcast_to(jnp.repeat(bias * LOG2E, ts)[:, None], (rows, pg))

    per_step = 2 if n_pages % 2 == 0 else 1
    n_steps = n_pages // per_step
    page_spec = lambda i: pl.BlockSpec(
        (None, None, pg, heads, hd), lambda b, j, pt: (layer, pt[b, n_pages - 1 - (j * per_step + i)], 0, 0, 0))
    new_map = lambda b, j, pt: (b, 0, 0, 0)
    grid_spec = pltpu.PrefetchScalarGridSpec(
        num_scalar_prefetch=1,
        grid=(bs, n_steps),
        in_specs=[pl.BlockSpec((ts, d), lambda b, j, pt: (b, 0)),
                  pl.BlockSpec((None, pg, heads, hd), new_map),
                  pl.BlockSpec((None, pg, heads, hd), new_map),
                  *[page_spec(i) for i in range(per_step)],
                  *[page_spec(i) for i in range(per_step)],
                  pl.BlockSpec((rows, pg), lambda b, j, pt: (0, 0))],
        out_specs=pl.BlockSpec((ts, d), lambda b, j, pt: (b, 0)),
        scratch_shapes=[pltpu.VMEM((n_tile, 2 * ts, 2 * hd), F32),
                        pltpu.VMEM((n_tile, 2 * ts, 2 * hd), F32),
                        pltpu.VMEM((rows, pg), F32),
                        *[pltpu.VMEM((heads * SB_HEAD_PITCH, hd), F32)] * (2 * per_step)],
    )
    return pl.pallas_call(
        functools.partial(_sb_sample_body, n_steps=n_steps, per_step=per_step, ts=ts),
        grid_spec=grid_spec,
        out_shape=jax.ShapeDtypeStruct((m, d), F32),
        compiler_params=_params(2),
        name="sb_attend_sample",
    )(page_table, q, pad(k_new), pad(v_new), *[cache_k] * per_step, *[cache_v] * per_step, brow)


def _mxu_operand(x):
    y = x.astype(BF16)
    return y if x.shape[0] >= 16 else y.astype(F32)


def _mlstm_body(q_ref, k_ref, v_ref, o_ref, ir_ref, ic_ref, fr_ref, fc_ref, hg_ref,
                c0_ref, n0_ref, m0_ref, y_ref, c_ref, n_ref, m_ref, ct, ns, ms, *, chunk, n_chunks):
    c = pl.program_id(1)
    ln = chunk
    heads, dk, dv = ct.shape

    @pl.when(c == 0)
    def _():
        for h in range(heads):
            ct[h] = c0_ref[h].T
        ns[...] = n0_ref[...]
        ms[...] = m0_ref[...]

    ti = lax.broadcasted_iota(jnp.int32, (ln, ln), 0)
    si = lax.broadcasted_iota(jnp.int32, (ln, ln), 1)
    causal = si <= ti

    for h in range(heads):
        qk_cols = slice(h * dk, (h + 1) * dk)
        v_cols = slice(h * dv, (h + 1) * dv)
        q = _mxu_operand(q_ref[:, qk_cols])
        k = _mxu_operand(k_ref[:, qk_cols])
        v = _mxu_operand(v_ref[:, v_cols])
        lf_r = _log_sigmoid(fr_ref[h])
        lf_c = _log_sigmoid(fc_ref[h])
        b_col = jnp.sum(jnp.where(causal, lf_r, 0.0), axis=1, keepdims=True)
        b_row = jnp.sum(jnp.where(ti <= si, lf_c, 0.0), axis=0, keepdims=True)
        dmat = jnp.where(causal, b_col - b_row + ir_ref[h], -jnp.inf)
        m_prev = ms[h]
        g = b_col + m_prev
        m_t = jnp.maximum(g, jnp.max(dmat, axis=1, keepdims=True))
        s = lax.dot_general(q, k, (((1,), (1,)), ((), ())), preferred_element_type=F32)
        w = jnp.exp(dmat - m_t) * s
        w_carry = jnp.exp(g - m_t)
        num = (w_carry * jnp.dot(q, _mxu_operand(ct[h]), preferred_element_type=F32)
               + jnp.dot(_mxu_operand(w), v, preferred_element_type=F32))
        qn = jnp.sum(q.astype(F32) * ns[h], axis=1, keepdims=True)
        den = w_carry * qn + jnp.sum(w, axis=1, keepdims=True)
        hh = num / jnp.maximum(jnp.abs(den), jnp.exp(-m_t))

        hms = jnp.mean(hh * hh, axis=1, keepdims=True)
        hn = hh * lax.rsqrt(hms + NORM_EPS) * hg_ref[:, v_cols]
        y_ref[:, v_cols] = (_sigmoid(o_ref[:, v_cols]) * hn).astype(y_ref.dtype)

        m_new = m_t[ln - 1:ln, :]
        b_last = b_col[ln - 1:ln, :]
        decay = jnp.exp(b_last - b_col + ic_ref[h] - m_new)
        c_scale = jnp.exp(b_last + m_prev - m_new)
        kd = decay * k.astype(F32)
        ct[h] = c_scale * ct[h] + lax.dot_general(
            _mxu_operand(kd), v, (((0,), (0,)), ((), ())), preferred_element_type=F32)
        ns[h] = c_scale * ns[h] + jnp.sum(kd, axis=0, keepdims=True)
        ms[h] = m_new

    @pl.when(c == n_chunks - 1)
    def _():
        for h in range(heads):
            c_ref[h] = ct[h].T
        n_ref[...] = ns[...]
        m_ref[...] = ms[...]


def mlstm(q, k, v, o, gates, h_gain, c0, n0, m0, *, batch, t_len, y_dtype):
    heads, dv, dk = c0.shape[1:]
    chunk = ML_CHUNK if t_len % ML_CHUNK == 0 else t_len
    nc = t_len // chunk
    gt = gates[:, :2 * heads].reshape(batch, nc, chunk, 2, heads)
    gt = jnp.transpose(gt, (3, 0, 4, 1, 2))
    i_row, f_row = gt[0][..., None, :], gt[1][..., None, :]
    i_col, f_col = gt[0][..., :, None], gt[1][..., :, None]

    tok = lambda width: pl.BlockSpec((chunk, heads * width), lambda b, c: (b * nc + c, 0))
    row = pl.BlockSpec((None, heads, None, 1, chunk), lambda b, c: (b, 0, c, 0, 0))
    col = pl.BlockSpec((None, heads, None, chunk, 1), lambda b, c: (b, 0, c, 0, 0))
    st = lambda r, w: pl.BlockSpec((None, heads, r, w), lambda b, c: (b, 0, 0, 0))
    return pl.pallas_call(
        functools.partial(_mlstm_body, chunk=chunk, n_chunks=nc),
        grid=(batch, nc),
        in_specs=[tok(dk), tok(dk), tok(dv), tok(dv), row, col, row, col,
                  pl.BlockSpec((1, heads * dv), lambda b, c: (0, 0)),
                  st(dv, dk), st(1, dk), st(1, 1)],
        out_specs=[tok(dv), st(dv, dk), st(1, dk), st(1, 1)],
        out_shape=[jax.ShapeDtypeStruct((batch * t_len, heads * dv), y_dtype),
                   jax.ShapeDtypeStruct((batch, heads, dv, dk), F32),
                   jax.ShapeDtypeStruct((batch, heads, 1, dk), F32),
                   jax.ShapeDtypeStruct((batch, heads, 1, 1), F32)],
        scratch_shapes=[pltpu.VMEM((heads, dk, dv), F32), pltpu.VMEM((heads, 1, dk), F32),
                        pltpu.VMEM((heads, 1, 1), F32)],
        compiler_params=_params(2),
        name="mlstm",
    )(q, k, v, o, i_row, i_col, f_row, f_col, h_gain.reshape(1, heads * dv),
      c0, n0.reshape(batch, heads, 1, dk), m0.reshape(batch, heads, 1, 1))


def _ffn_half(xp, xs, gain, w_gate, w_up, w_down, prefix):
    hp, hs = rmsnorm(xp, gain), rmsnorm(xs, gain)
    d_ff = w_gate.shape[-1]
    ap, as_ = matmul(hp, hs, (w_gate, w_up), prefix=prefix, n_out=d_ff, epi="swiglu", out_dtype=BF16)
    return matmul(ap, as_, (w_down,), prefix=prefix, n_out=xp.shape[1], epi="residual", coef=0.5,
                  out_dtype=F32, row_extras=((xp, xs),))


def kernel(x_prompt, x_sample, cache_k, cache_v, state_C, state_n, state_m, page_table, norm_g, w_ffn_gate, w_ffn_up, w_ffn_down, sb_w_in, sb_q_norm, sb_k_norm, sb_logit_bias, sb_w_out, ml_w_in, ml_b_gates, ml_h_norm, ml_w_out):
    bp, t_len, d = x_prompt.shape
    bs, ts, _ = x_sample.shape
    depth = norm_g.shape[0]
    sb_heads = sb_logit_bias.shape[1]
    sb_hd = d // sb_heads
    assert sb_hd == V7X_LANES
    ml_heads, ml_dv, ml_dk = state_C.shape[2:]

    xp = x_prompt.reshape(bp * t_len, d)
    xs = x_sample.reshape(bs * ts, d)
    w_down = w_ffn_down
    kp_rows, vp_rows, ks_rows, vs_rows = [], [], [], []
    c_p, n_p, m_p, c_s, n_s, m_s = [], [], [], [], [], []

    for l in range(depth):
        xp, xs = _ffn_half(xp, xs, norm_g[l, 0], w_ffn_gate, w_ffn_up, w_down, (l, 0))
        hp, hs = rmsnorm(xp, norm_g[l, 1]), rmsnorm(xs, norm_g[l, 1])
        if l % DEPTH_MIXERS == 0:
            a = l // DEPTH_MIXERS
            tile = lambda g: jnp.tile(g, d // sb_hd).reshape(1, d)
            qp, qs = matmul(hp, hs, (sb_w_in,), prefix=(a,), n_out=d, col_off=0, epi="headnorm",
                            coef=LOG2E / math.sqrt(sb_hd), out_dtype=BF16, col_extras=(tile(sb_q_norm[a]),))
            kp, ks = matmul(hp, hs, (sb_w_in,), prefix=(a,), n_out=d, col_off=d, epi="headnorm",
                            out_dtype=F32, col_extras=(tile(sb_k_norm[a]),))
            vp, vs = matmul(hp, hs, (sb_w_in,), prefix=(a,), n_out=d, col_off=2 * d, epi="cast",
                            out_dtype=F32)
            att_p = sb_attend_prompt(qp, kp, vp, sb_logit_bias[a], batch=bp, t_len=t_len)
            att_s = sb_attend_sample(qs.astype(F32), ks, vs, cache_k, cache_v, page_table,
                                     sb_logit_bias[a], layer=a, ts=ts)
            xp, xs = matmul(att_p, att_s.astype(BF16), (sb_w_out,), prefix=(a,), n_out=d, epi="residual",
                            out_dtype=F32, row_extras=((xp, xs),))
            kp_rows.append(kp.reshape(bp, t_len, sb_heads, sb_hd))
            vp_rows.append(vp.reshape(bp, t_len, sb_heads, sb_hd))
            ks_rows.append(ks.reshape(bs, ts, sb_heads, sb_hd))
            vs_rows.append(vs.reshape(bs, ts, sb_heads, sb_hd))
        else:
            jl = l // DEPTH_MIXERS
            s_qk = ml_heads * ml_dk
            w_in_nk = jnp.swapaxes(ml_w_in, 1, 2)
            proj = functools.partial(matmul, hp, hs, (w_in_nk,), prefix=(jl,), w_is_nk=True)
            qp, qs = proj(n_out=s_qk, col_off=0, epi="cast", out_dtype=BF16)
            kp, ks = proj(n_out=s_qk, col_off=s_qk, epi="cast", coef=ml_dk ** -0.5, out_dtype=BF16)
            vp, vs = proj(n_out=d, col_off=2 * s_qk, epi="cast", out_dtype=BF16)
            op, os_ = proj(n_out=d, col_off=2 * s_qk + d, epi="cast", out_dtype=F32)
            n_gate = 2 * ml_heads
            w_g = jnp.pad(w_in_nk[jl:jl + 1, 2 * s_qk + 2 * d:, :], ((0, 0), (0, V7X_LANES - n_gate), (0, 0)))
            b_g = jnp.pad(ml_b_gates[jl].reshape(1, n_gate), ((0, 0), (0, V7X_LANES - n_gate)))
            gp, gs = matmul(hp, hs, (w_g,), prefix=(0,), n_out=V7X_LANES, epi="bias", out_dtype=F32,
                            col_extras=(b_g,), tn=V7X_LANES, w_is_nk=True)

            zeros = lambda *shape: jnp.zeros(shape, F32)
            yp, c, n, m = mlstm(qp, kp, vp, op, gp, ml_h_norm[jl], zeros(bp, ml_heads, ml_dv, ml_dk),
                                zeros(bp, ml_heads, ml_dk), zeros(bp, ml_heads),
                                batch=bp, t_len=t_len, y_dtype=BF16)
            c_p.append(c); n_p.append(n.reshape(bp, ml_heads, ml_dk)); m_p.append(m.reshape(bp, ml_heads))
            f32 = lambda a_: a_.astype(F32)
            ys, c, n, m = mlstm(f32(qs), f32(ks), f32(vs), os_, gs, ml_h_norm[jl], state_C[jl], state_n[jl],
                                state_m[jl], batch=bs, t_len=ts, y_dtype=F32)
            c_s.append(c); n_s.append(n.reshape(bs, ml_heads, ml_dk)); m_s.append(m.reshape(bs, ml_heads))
            xp, xs = matmul(yp, ys.astype(BF16), (ml_w_out,), prefix=(jl,), n_out=d, epi="residual",
                            out_dtype=F32, row_extras=((xp, xs),))
        xp, xs = _ffn_half(xp, xs, norm_g[l, 2], w_ffn_gate, w_ffn_up, w_down, (l, 1))

    stack = lambda rows: rows[0][None] if len(rows) == 1 else jnp.stack(rows)
    return (xp.reshape(bp, t_len, d), xs.reshape(bs, ts, d),
            stack(kp_rows), stack(vp_rows), stack(c_p), stack(n_p), stack(m_p),
            stack(ks_rows), stack(vs_rows), stack(c_s), stack(n_s), stack(m_s))
```

```python
import functools
import math

import jax
import jax.numpy as jnp
from jax import lax
from jax.experimental import pallas as pl
from jax.experimental.pallas import tpu as pltpu

F32 = jnp.float32
BF16 = jnp.bfloat16

NORM_EPS = 1e-6
ML_CHUNK = 128
DEPTH_MIXERS = 2

V7X_VMEM_BYTES = 64 * 1024 * 1024
V7X_LANES = 128
V7X_MXU_DIM = 256
VMEM_LIMIT_BYTES = V7X_VMEM_BYTES - 6 * 1024 * 1024

LOG2E = math.log2(math.e)
MASK_NEG = -1e30
SB_KEY_BLOCK = 128
SB_ROW_CHUNK = 256


def _params(n_axes):
    return pltpu.CompilerParams(dimension_semantics=("arbitrary",) * n_axes,
                                vmem_limit_bytes=VMEM_LIMIT_BYTES)


def _largest_tile(m, cap):
    t = min(m, cap)
    while m % t:
        t -= 8
    return t


def _log_sigmoid(x):
    return jnp.minimum(x, 0.0) - jnp.log(1.0 + jnp.exp(-jnp.abs(x)))


def _log2_sigmoid(x2):
    neg_abs = lax.bitcast_convert_type(lax.bitcast_convert_type(x2, jnp.uint32) | jnp.uint32(1 << 31), F32)
    return jnp.minimum(x2, 0.0) - jnp.log2(1.0 + jnp.exp2(neg_abs))


def _sigmoid(x):
    return 1.0 / (1.0 + jnp.exp(-x))


def _split_bf16(x):
    hi = x.astype(BF16)
    lo = (x - hi.astype(F32)).astype(BF16)
    return hi, lo


def _rmsnorm_body(x_ref, g_ref, o_ref):
    x = x_ref[...]
    ms = jnp.mean(x * x, axis=-1, keepdims=True)
    o_ref[...] = (x * lax.rsqrt(ms + NORM_EPS) * g_ref[...]).astype(o_ref.dtype)


def rmsnorm(x, g):
    m, d = x.shape
    tm = _largest_tile(m, 512)
    return pl.pallas_call(
        _rmsnorm_body,
        grid=(m // tm,),
        in_specs=[pl.BlockSpec((tm, d), lambda i: (i, 0)),
                  pl.BlockSpec((1, d), lambda i: (0, 0))],
        out_specs=pl.BlockSpec((tm, d), lambda i: (i, 0)),
        out_shape=jax.ShapeDtypeStruct((m, d), BF16),
        compiler_params=_params(1),
        name="rmsnorm",
    )(x, g.reshape(1, d))


def _epilogue(accs, extras, *, epi, coef, out_dtype):
    if epi == "swiglu":
        g, u = accs
        return (g * _sigmoid(g) * u).astype(out_dtype)
    (acc,) = accs
    if epi == "cast":
        return (acc * coef).astype(out_dtype) if coef != 1.0 else acc.astype(out_dtype)
    if epi == "bias":
        (b,) = extras
        return (acc + b).astype(out_dtype)
    if epi == "headnorm":
        (gain,) = extras
        outs = []
        for h in range(acc.shape[1] // V7X_LANES):
            a = acc[:, h * V7X_LANES:(h + 1) * V7X_LANES]
            ms = jnp.mean(a * a, axis=-1, keepdims=True)
            y = a * lax.rsqrt(ms + NORM_EPS) * gain[:, h * V7X_LANES:(h + 1) * V7X_LANES]
            outs.append(y * coef if coef != 1.0 else y)
        return jnp.concatenate(outs, axis=1).astype(out_dtype)
    if epi == "residual":
        (x,) = extras
        return (x + coef * acc).astype(out_dtype)
    raise ValueError(epi)


def _mm_body(*refs, n_w, n_row_extra, n_col_extra, epi, coef, w_is_nk):
    ap_ref, as_ref = refs[0], refs[1]
    contract = (((1,), (1 if w_is_nk else 0,)), ((), ()))
    w_refs = refs[2:2 + n_w]
    pos = 2 + n_w
    xp_refs = refs[pos:pos + n_row_extra]
    xs_refs = refs[pos + n_row_extra:pos + 2 * n_row_extra]
    pos += 2 * n_row_extra
    c_refs = refs[pos:pos + n_col_extra]
    op_ref, os_ref = refs[pos + n_col_extra:]

    wb = [w[...].astype(BF16) for w in w_refs]
    col = [c[...] for c in c_refs]

    def run(a_ref, x_refs, o_ref):
        a = a_ref[...]
        accs = [lax.dot_general(a, w, contract, preferred_element_type=F32) for w in wb]
        extras = [x[...] for x in x_refs] + col
        o_ref[...] = _epilogue(accs, extras, epi=epi, coef=coef, out_dtype=o_ref.dtype)

    run(ap_ref, xp_refs, op_ref)

    @pl.when(pl.program_id(0) == 0)
    def _():
        run(as_ref, xs_refs, os_ref)


def _mm_tiles(mp, k, n_out, n_w, tn):
    if tn is not None:
        return _largest_tile(mp, 2048), tn, True
    if n_w == 1 and k <= 4096 and n_out % (2 * V7X_MXU_DIM) == 0:
        return _largest_tile(mp, 1024), 2 * V7X_MXU_DIM, False
    return _largest_tile(mp, 2048 if k <= 4096 else 1024), V7X_MXU_DIM, True


def matmul(ap, as_, ws, *, prefix, n_out, col_off=0, epi, coef=1.0, out_dtype,
           row_extras=(), col_extras=(), tn=None, w_is_nk=False):
    mp, k = ap.shape
    ms = as_.shape[0]
    tm, tn, single_a = _mm_tiles(mp, k, n_out, len(ws), tn)
    ni, nj = mp // tm, n_out // tn
    assert n_out % tn == 0 and mp % tm == 0 and col_off % tn == 0
    npre = len(prefix)
    w_block = (tn, k) if w_is_nk else (k, tn)

    def w_map(i, j):
        jw = j + col_off // tn
        return (*prefix, jw, 0) if w_is_nk else (*prefix, 0, jw)

    def s_col(i, j):
        return (0, jnp.where(i == 0, j, nj - 1))

    in_specs = [pl.BlockSpec((tm, k), lambda i, j: (i, 0), pipeline_mode=pl.Buffered(1 if single_a else 2)),
                pl.BlockSpec((ms, k), lambda i, j: (0, 0))]
    in_specs += [pl.BlockSpec((None,) * npre + w_block, w_map) for _ in ws]
    in_specs += [pl.BlockSpec((tm, tn), lambda i, j: (i, j)) for _ in row_extras]
    in_specs += [pl.BlockSpec((ms, tn), s_col) for _ in row_extras]
    in_specs += [pl.BlockSpec((1, tn), lambda i, j: (0, j)) for _ in col_extras]
    args = [ap, as_, *ws, *[e[0] for e in row_extras], *[e[1] for e in row_extras], *col_extras]

    body = functools.partial(_mm_body, n_w=len(ws), n_row_extra=len(row_extras),
                             n_col_extra=len(col_extras), epi=epi, coef=coef, w_is_nk=w_is_nk)
    return pl.pallas_call(
        body,
        grid=(ni, nj),
        in_specs=in_specs,
        out_specs=[pl.BlockSpec((tm, tn), lambda i, j: (i, j)),
                   pl.BlockSpec((ms, tn), s_col)],
        out_shape=[jax.ShapeDtypeStruct((mp, n_out), out_dtype),
                   jax.ShapeDtypeStruct((ms, n_out), out_dtype)],
        compiler_params=_params(2),
        name="mm_" + epi,
    )(*args)


SB_UNROLL = 8
SB_LAG = 1


def _sb_schedule(t_len):
    kb_sz, rc = SB_KEY_BLOCK, SB_ROW_CHUNK
    r_per = rc // kb_sz
    items = [(kb, c) for kb in reversed(range(t_len // kb_sz)) for c in range(kb // r_per, t_len // rc)]
    n_trip = -(-(len(items) + 5 * SB_LAG) // SB_UNROLL) * SB_UNROLL
    spare = (0, t_len, r_per)

    def entry(i):
        if not 0 <= i < len(items):
            return spare
        kb, c = items[i]
        return kb, c * rc, (kb - c * r_per) if c == kb // r_per else r_per

    cols = [[entry(t - lag * SB_LAG)[f] for t in range(n_trip)] for lag, f in
            ((0, 0), (0, 1), (1, 2), (3, 1), (4, 0), (5, 1))]
    cols[1] = [min(r, t_len - rc) for r in cols[1]]
    return n_trip, jnp.asarray(cols, jnp.int32)


def _sb_prompt_body(tbl_ref, q_ref, k_ref, v_ref, bias_ref, o_ref,
                    ktd, vbd, acc, carry, bvar, *rings, t_len, n_trip):
    z_raw, s_hl, cs_raw, s_w, o_raw = (rings[2 * i:2 * i + 2] for i in range(5))
    s_ls = rings[10:10 + SB_UNROLL]
    kb_sz, rc = SB_KEY_BLOCK, SB_ROW_CHUNK
    nkb, r_per = t_len // kb_sz, rc // kb_sz
    hd = V7X_LANES

    @pl.when((pl.program_id(0) == 0) & (pl.program_id(1) == 0))
    def _():
        ktd[...] = jnp.zeros_like(ktd)
        vbd[...] = jnp.zeros_like(vbd)

    for kb in range(nkb):
        rows = slice(kb * kb_sz, (kb + 1) * kb_sz)
        for e in range(2):
            cols = slice(e * hd, (e + 1) * hd)
            ktd[kb, e * hd:(e + 1) * hd, e * kb_sz:(e + 1) * kb_sz] = k_ref[rows, cols].T.astype(BF16)
            vbd[kb, e * kb_sz:(e + 1) * kb_sz, cols] = v_ref[rows, cols].astype(BF16)

    for buf in (acc, carry, *rings):
        buf[...] = jnp.zeros_like(buf)

    bias = bias_ref[...]
    ri = lax.broadcasted_iota(jnp.int32, (rc, 2 * kb_sz), 0)
    si = lax.broadcasted_iota(jnp.int32, (rc, 2 * kb_sz), 1) & (kb_sz - 1)
    for d in range(r_per):
        bvar[d] = jnp.where(d * kb_sz + si < ri, bias, MASK_NEG)
    bvar[r_per] = jnp.broadcast_to(bias, (rc, 2 * kb_sz))

    jr = lax.broadcasted_iota(jnp.int32, (4 * kb_sz, 2 * kb_sz), 0) & (2 * kb_sz - 1)
    jc = lax.broadcasted_iota(jnp.int32, (4 * kb_sz, 2 * kb_sz), 1)
    same_head = (jr // kb_sz) == (jc // kb_sz)
    tri = jnp.where(same_head & ((jr > jc) | ((jc & (kb_sz - 1)) == 0)), 1.0, 0.0).astype(BF16)
    first_key = (lax.broadcasted_iota(jnp.int32, (rc, 2 * kb_sz), 1) & (kb_sz - 1)) == 0

    def trip(t, phase):
        cur, prev = phase % 2, 1 - phase % 2
        rows = lambda r: pl.ds(pl.multiple_of(tbl_ref[r, t], rc), rc)
        z_raw[cur][...] = jnp.dot(q_ref[rows(1), :], ktd[tbl_ref[0, t]], preferred_element_type=F32)
        z = z_raw[prev][...] + bvar[tbl_ref[2, t]]
        ls = _log2_sigmoid(z)
        hi, lo = _split_bf16(ls - z)
        s_ls[phase][...] = jnp.where(first_key, z, ls)
        s_hl[cur][...] = jnp.concatenate([hi, lo], axis=1)
        cs_raw[cur][...] = jnp.dot(s_hl[prev][...], tri, preferred_element_type=F32)
        rows_e2 = rows(3)
        cs = cs_raw[prev][...]
        s_w[cur][...] = jnp.exp2(s_ls[(phase - 2) % SB_UNROLL][...] + cs + carry[rows_e2, :]).astype(BF16)
        carry[rows_e2, :] += jnp.concatenate(
            [jnp.broadcast_to(cs[:, e * kb_sz:e * kb_sz + 1], (rc, kb_sz)) for e in range(2)], axis=1)
        o_raw[cur][...] = jnp.dot(s_w[prev][...], vbd[tbl_ref[4, t]], preferred_element_type=F32)
        acc[rows(5), :] += o_raw[prev][...]

    def trips(u, _):
        for phase in range(SB_UNROLL):
            trip(SB_UNROLL * u + phase, phase)
        return 0

    lax.fori_loop(0, n_trip // SB_UNROLL, trips, 0)
    o_ref[...] = acc[0:t_len, :].astype(o_ref.dtype)


def sb_attend_prompt(q, k, v, bias, *, batch, t_len):
    m, d = q.shape
    w2 = 2 * V7X_LANES
    n_pair = d // w2
    nkb = t_len // SB_KEY_BLOCK
    rc = SB_ROW_CHUNK
    assert t_len % rc == 0
    n_trip, tables = _sb_schedule(t_len)
    spec = pl.BlockSpec((t_len, w2), lambda b, p, tbl: (b, p))
    grid_spec = pltpu.PrefetchScalarGridSpec(
        num_scalar_prefetch=1,
        grid=(batch, n_pair),
        in_specs=[spec, spec, spec, pl.BlockSpec((1, w2), lambda b, p, tbl: (0, p))],
        out_specs=spec,
        scratch_shapes=[pltpu.VMEM((nkb, w2, w2), BF16),
                        pltpu.VMEM((nkb, w2, w2), BF16),
                        pltpu.VMEM((t_len + rc, w2), F32),
                        pltpu.VMEM((t_len + rc, w2), F32),
                        pltpu.VMEM((rc // SB_KEY_BLOCK + 1, rc, w2), F32),
                        *[pltpu.VMEM((rc, w2), F32)] * 2,
                        *[pltpu.VMEM((rc, 2 * w2), BF16)] * 2,
                        *[pltpu.VMEM((rc, w2), F32)] * 2,
                        *[pltpu.VMEM((rc, w2), BF16)] * 2,
                        *[pltpu.VMEM((rc, w2), F32)] * 2,
                        *[pltpu.VMEM((rc, w2), F32)] * SB_UNROLL],
    )
    return pl.pallas_call(
        functools.partial(_sb_prompt_body, t_len=t_len, n_trip=n_trip),
        grid_spec=grid_spec,
        out_shape=jax.ShapeDtypeStruct((m, d), BF16),
        compiler_params=_params(2),
        name="sb_attend_prompt",
    )(tables, q, k, v, jnp.repeat(bias * LOG2E, V7X_LANES).reshape(1, d))


SB_HEAD_PITCH = SB_KEY_BLOCK + 4


def _sb_sample_body(pt_ref, q_ref, knew_ref, vnew_ref, *refs, n_steps, per_step, ts):
    kpg_refs, vpg_refs = refs[:per_step], refs[per_step:2 * per_step]
    brow_ref, o_ref, qbd, acc, carry = refs[2 * per_step:2 * per_step + 5]
    hm_bufs = refs[2 * per_step + 5:]
    j = pl.program_id(1)
    hd, pg, pitch = V7X_LANES, SB_KEY_BLOCK, SB_HEAD_PITCH
    n_tile = qbd.shape[0]
    rows_per = 2 * ts

    jr = lax.broadcasted_iota(jnp.int32, (2 * pg, 2 * pg), 0) & (pg - 1)
    jc = lax.broadcasted_iota(jnp.int32, (2 * pg, 2 * pg), 1)
    tri_ones = jnp.where((jc >= pg) | (jr > jc), 1.0, 0.0).astype(BF16)

    def regroup(src_ref, dst):
        for p in range(pg):
            for g in range(src_ref.shape[1] // 8):
                dst[pl.ds(8 * g * pitch + p, 8, stride=pitch), :] = src_ref[p, 8 * g:8 * g + 8, :]

    def head_pair(buf, n):
        return jnp.concatenate([buf[h * pitch:h * pitch + pg, :].astype(BF16) for h in (2 * n, 2 * n + 1)],
                               axis=1)

    def process(k_ref, v_ref, khm, vhm, masked):
        regroup(k_ref, khm)
        regroup(v_ref, vhm)
        zs = []
        for n in range(n_tile):
            zs.append(lax.dot_general(qbd[n].astype(BF16), head_pair(khm, n), (((1,), (1,)), ((), ())),
                                      preferred_element_type=F32))
        z = jnp.concatenate(zs, axis=0) + brow_ref[...]
        if masked:
            qi = lax.broadcasted_iota(jnp.int32, z.shape, 0) % ts
            si = lax.broadcasted_iota(jnp.int32, z.shape, 1)
            z = jnp.where(si < qi, z, MASK_NEG)
        ls = _log2_sigmoid(z)
        hi, lo = _split_bf16(ls - z)
        cs = jnp.dot(jnp.concatenate([hi, lo], axis=1), tri_ones,
                     preferred_element_type=F32)
        w = jnp.exp2(ls + cs[:, :pg] + carry[...]).astype(BF16)
        for n in range(n_tile):
            acc[n] += jnp.dot(w[n * rows_per:(n + 1) * rows_per, :], head_pair(vhm, n),
                              preferred_element_type=F32)
        carry[...] += cs[:, pg:]

    @pl.when(j == 0)
    def _():
        qbd[...] = jnp.zeros_like(qbd)
        for n in range(n_tile):
            for e in range(2):
                qbd[n, e * ts:(e + 1) * ts, e * hd:(e + 1) * hd] = \
                    q_ref[:, (2 * n + e) * hd:(2 * n + e + 1) * hd]
        acc[...] = jnp.zeros_like(acc)
        carry[...] = jnp.zeros_like(carry)
        process(knew_ref, vnew_ref, hm_bufs[0], hm_bufs[1], True)

    n_hm = len(hm_bufs) // 2
    for i, (k_ref, v_ref) in enumerate(zip(kpg_refs, vpg_refs)):
        process(k_ref, v_ref, hm_bufs[2 * (i % n_hm)], hm_bufs[2 * (i % n_hm) + 1], False)

    @pl.when(j == n_steps - 1)
    def _():
        for n in range(n_tile):
            for e in range(2):
                o_ref[:, (2 * n + e) * hd:(2 * n + e + 1) * hd] = \
                    acc[n, e * ts:(e + 1) * ts, e * hd:(e + 1) * hd]


def sb_attend_sample(q, k_new, v_new, cache_k, cache_v, page_table, bias, *, layer, ts):
    m, d = q.shape
    bs = m // ts
    n_pages = page_table.shape[1]
    n_layers, n_phys, pg = cache_k.shape[:3]
    assert pg == SB_KEY_BLOCK and ts % 8 == 0
    heads, hd = cache_k.shape[3:]
    assert hd == V7X_LANES and heads * hd == d and heads % 8 == 0
    n_tile = heads // 2
    rows = heads * ts
    pad = lambda a: jnp.pad(a.reshape(bs, ts, heads, hd), ((0, 0), (0, pg - ts), (0, 0), (0, 0)))
    brow = jnp.broadcast_to(jnp.repeat(bias * LOG2E, ts)[:, None], (rows, pg))

    per_step = max(p for p in (4, 2, 1) if n_pages % p == 0)
    n_steps = n_pages // per_step
    page_spec = lambda i: pl.BlockSpec(
        (None, None, pg, heads, hd), lambda b, j, pt: (layer, pt[b, n_pages - 1 - (j * per_step + i)], 0, 0, 0))
    new_map = lambda b, j, pt: (b, 0, 0, 0)
    grid_spec = pltpu.PrefetchScalarGridSpec(
        num_scalar_prefetch=1,
        grid=(bs, n_steps),
        in_specs=[pl.BlockSpec((ts, d), lambda b, j, pt: (b, 0)),
                  pl.BlockSpec((None, pg, heads, hd), new_map),
                  pl.BlockSpec((None, pg, heads, hd), new_map),
                  *[page_spec(i) for i in range(per_step)],
                  *[page_spec(i) for i in range(per_step)],
                  pl.BlockSpec((rows, pg), lambda b, j, pt: (0, 0))],
        out_specs=pl.BlockSpec((ts, d), lambda b, j, pt: (b, 0)),
        scratch_shapes=[pltpu.VMEM((n_tile, 2 * ts, 2 * hd), F32),
                        pltpu.VMEM((n_tile, 2 * ts, 2 * hd), F32),
                        pltpu.VMEM((rows, pg), F32),
                        *[pltpu.VMEM((heads * SB_HEAD_PITCH, hd), F32)] * (2 * min(per_step, 2))],
    )
    return pl.pallas_call(
        functools.partial(_sb_sample_body, n_steps=n_steps, per_step=per_step, ts=ts),
        grid_spec=grid_spec,
        out_shape=jax.ShapeDtypeStruct((m, d), F32),
        compiler_params=_params(2),
        name="sb_attend_sample",
    )(page_table, q, pad(k_new), pad(v_new), *[cache_k] * per_step, *[cache_v] * per_step, brow)


def _mxu_operand(x):
    y = x.astype(BF16)
    return y if x.shape[0] >= 16 else y.astype(F32)


def _mlstm_body(q_ref, k_ref, v_ref, o_ref, ir_ref, ic_ref, fr_ref, fc_ref, hg_ref,
                c0_ref, n0_ref, m0_ref, y_ref, c_ref, n_ref, m_ref, ct, ns, ms, *, chunk, n_chunks):
    c = pl.program_id(1)
    ln = chunk
    heads, dk, dv = ct.shape

    @pl.when(c == 0)
    def _():
        for h in range(heads):
            ct[h] = c0_ref[h].T
        ns[...] = n0_ref[...]
        ms[...] = m0_ref[...]

    ti = lax.broadcasted_iota(jnp.int32, (ln, ln), 0)
    si = lax.broadcasted_iota(jnp.int32, (ln, ln), 1)
    causal = si <= ti

    for h in range(heads):
        qk_cols = slice(h * dk, (h + 1) * dk)
        v_cols = slice(h * dv, (h + 1) * dv)
        q = _mxu_operand(q_ref[:, qk_cols])
        k = _mxu_operand(k_ref[:, qk_cols])
        v = _mxu_operand(v_ref[:, v_cols])
        lf_r = _log_sigmoid(fr_ref[h])
        lf_c = _log_sigmoid(fc_ref[h])
        b_col = jnp.sum(jnp.where(causal, lf_r, 0.0), axis=1, keepdims=True)
        b_row = jnp.sum(jnp.where(ti <= si, lf_c, 0.0), axis=0, keepdims=True)
        dmat = jnp.where(causal, b_col - b_row + ir_ref[h], -jnp.inf)
        m_prev = ms[h]
        g = b_col + m_prev
        m_t = jnp.maximum(g, jnp.max(dmat, axis=1, keepdims=True))
        s = lax.dot_general(q, k, (((1,), (1,)), ((), ())), preferred_element_type=F32)
        w = jnp.exp(dmat - m_t) * s
        w_carry = jnp.exp(g - m_t)
        num = (w_carry * jnp.dot(q, _mxu_operand(ct[h]), preferred_element_type=F32)
               + jnp.dot(_mxu_operand(w), v, preferred_element_type=F32))
        qn = jnp.sum(q.astype(F32) * ns[h], axis=1, keepdims=True)
        den = w_carry * qn + jnp.sum(w, axis=1, keepdims=True)
        hh = num / jnp.maximum(jnp.abs(den), jnp.exp(-m_t))

        hms = jnp.mean(hh * hh, axis=1, keepdims=True)
        hn = hh * lax.rsqrt(hms + NORM_EPS) * hg_ref[:, v_cols]
        y_ref[:, v_cols] = (_sigmoid(o_ref[:, v_cols]) * hn).astype(y_ref.dtype)

        m_new = m_t[ln - 1:ln, :]
        b_last = b_col[ln - 1:ln, :]
        decay = jnp.exp(b_last - b_col + ic_ref[h] - m_new)
        c_scale = jnp.exp(b_last + m_prev - m_new)
        kd = decay * k.astype(F32)
        ct[h] = c_scale * ct[h] + lax.dot_general(
            _mxu_operand(kd), v, (((0,), (0,)), ((), ())), preferred_element_type=F32)
        ns[h] = c_scale * ns[h] + jnp.sum(kd, axis=0, keepdims=True)
        ms[h] = m_new

    @pl.when(c == n_chunks - 1)
    def _():
        for h in range(heads):
            c_ref[h] = ct[h].T
        n_ref[...] = ns[...]
        m_ref[...] = ms[...]


def mlstm(q, k, v, o, gates, h_gain, c0, n0, m0, *, batch, t_len, y_dtype):
    heads, dv, dk = c0.shape[1:]
    chunk = ML_CHUNK if t_len % ML_CHUNK == 0 else t_len
    nc = t_len // chunk
    gt = gates[:, :2 * heads].reshape(batch, nc, chunk, 2, heads)
    gt = jnp.transpose(gt, (3, 0, 4, 1, 2))
    i_row, f_row = gt[0][..., None, :], gt[1][..., None, :]
    i_col, f_col = gt[0][..., :, None], gt[1][..., :, None]

    tok = lambda width: pl.BlockSpec((chunk, heads * width), lambda b, c: (b * nc + c, 0))
    row = pl.BlockSpec((None, heads, None, 1, chunk), lambda b, c: (b, 0, c, 0, 0))
    col = pl.BlockSpec((None, heads, None, chunk, 1), lambda b, c: (b, 0, c, 0, 0))
    st = lambda r, w: pl.BlockSpec((None, heads, r, w), lambda b, c: (b, 0, 0, 0))
    return pl.pallas_call(
        functools.partial(_mlstm_body, chunk=chunk, n_chunks=nc),
        grid=(batch, nc),
        in_specs=[tok(dk), tok(dk), tok(dv), tok(dv), row, col, row, col,
                  pl.BlockSpec((1, heads * dv), lambda b, c: (0, 0)),
                  st(dv, dk), st(1, dk), st(1, 1)],
        out_specs=[tok(dv), st(dv, dk), st(1, dk), st(1, 1)],
        out_shape=[jax.ShapeDtypeStruct((batch * t_len, heads * dv), y_dtype),
                   jax.ShapeDtypeStruct((batch, heads, dv, dk), F32),
                   jax.ShapeDtypeStruct((batch, heads, 1, dk), F32),
                   jax.ShapeDtypeStruct((batch, heads, 1, 1), F32)],
        scratch_shapes=[pltpu.VMEM((heads, dk, dv), F32), pltpu.VMEM((heads, 1, dk), F32),
                        pltpu.VMEM((heads, 1, 1), F32)],
        compiler_params=_params(2),
        name="mlstm",
    )(q, k, v, o, i_row, i_col, f_row, f_col, h_gain.reshape(1, heads * dv),
      c0, n0.reshape(batch, heads, 1, dk), m0.reshape(batch, heads, 1, 1))


def _ffn_half(xp, xs, gain, w_gate, w_up, w_down, prefix):
    hp, hs = rmsnorm(xp, gain), rmsnorm(xs, gain)
    d_ff = w_gate.shape[-1]
    ap, as_ = matmul(hp, hs, (w_gate, w_up), prefix=prefix, n_out=d_ff, epi="swiglu", out_dtype=BF16)
    return matmul(ap, as_, (w_down,), prefix=prefix, n_out=xp.shape[1], epi="residual", coef=0.5,
                  out_dtype=F32, row_extras=((xp, xs),))


def kernel(x_prompt, x_sample, cache_k, cache_v, state_C, state_n, state_m, page_table, norm_g, w_ffn_gate, w_ffn_up, w_ffn_down, sb_w_in, sb_q_norm, sb_k_norm, sb_logit_bias, sb_w_out, ml_w_in, ml_b_gates, ml_h_norm, ml_w_out):
    bp, t_len, d = x_prompt.shape
    bs, ts, _ = x_sample.shape
    depth = norm_g.shape[0]
    sb_heads = sb_logit_bias.shape[1]
    sb_hd = d // sb_heads
    assert sb_hd == V7X_LANES
    ml_heads, ml_dv, ml_dk = state_C.shape[2:]

    xp = x_prompt.reshape(bp * t_len, d)
    xs = x_sample.reshape(bs * ts, d)
    w_down = w_ffn_down
    kp_rows, vp_rows, ks_rows, vs_rows = [], [], [], []
    c_p, n_p, m_p, c_s, n_s, m_s = [], [], [], [], [], []

    for l in range(depth):
        xp, xs = _ffn_half(xp, xs, norm_g[l, 0], w_ffn_gate, w_ffn_up, w_down, (l, 0))
        hp, hs = rmsnorm(xp, norm_g[l, 1]), rmsnorm(xs, norm_g[l, 1])
        if l % DEPTH_MIXERS == 0:
            a = l // DEPTH_MIXERS
            tile = lambda g: jnp.tile(g, d // sb_hd).reshape(1, d)
            qp, qs = matmul(hp, hs, (sb_w_in,), prefix=(a,), n_out=d, col_off=0, epi="headnorm",
                            coef=LOG2E / math.sqrt(sb_hd), out_dtype=BF16, col_extras=(tile(sb_q_norm[a]),))
            kp, ks = matmul(hp, hs, (sb_w_in,), prefix=(a,), n_out=d, col_off=d, epi="headnorm",
                            out_dtype=F32, col_extras=(tile(sb_k_norm[a]),))
            vp, vs = matmul(hp, hs, (sb_w_in,), prefix=(a,), n_out=d, col_off=2 * d, epi="cast",
                            out_dtype=F32)
            att_p = sb_attend_prompt(qp, kp, vp, sb_logit_bias[a], batch=bp, t_len=t_len)
            att_s = sb_attend_sample(qs.astype(F32), ks, vs, cache_k, cache_v, page_table,
                                     sb_logit_bias[a], layer=a, ts=ts)
            xp, xs = matmul(att_p, att_s.astype(BF16), (sb_w_out,), prefix=(a,), n_out=d, epi="residual",
                            out_dtype=F32, row_extras=((xp, xs),))
            kp_rows.append(kp.reshape(bp, t_len, sb_heads, sb_hd))
            vp_rows.append(vp.reshape(bp, t_len, sb_heads, sb_hd))
            ks_rows.append(ks.reshape(bs, ts, sb_heads, sb_hd))
            vs_rows.append(vs.reshape(bs, ts, sb_heads, sb_hd))
        else:
            jl = l // DEPTH_MIXERS
            s_qk = ml_heads * ml_dk
            w_in_nk = jnp.swapaxes(ml_w_in, 1, 2)
            proj = functools.partial(matmul, hp, hs, (w_in_nk,), prefix=(jl,), w_is_nk=True)
            qp, qs = proj(n_out=s_qk, col_off=0, epi="cast", out_dtype=BF16)
            kp, ks = proj(n_out=s_qk, col_off=s_qk, epi="cast", coef=ml_dk ** -0.5, out_dtype=BF16)
            vp, vs = proj(n_out=d, col_off=2 * s_qk, epi="cast", out_dtype=BF16)
            op, os_ = proj(n_out=d, col_off=2 * s_qk + d, epi="cast", out_dtype=F32)
            n_gate = 2 * ml_heads
            w_g = jnp.pad(w_in_nk[jl:jl + 1, 2 * s_qk + 2 * d:, :], ((0, 0), (0, V7X_LANES - n_gate), (0, 0)))
            b_g = jnp.pad(ml_b_gates[jl].reshape(1, n_gate), ((0, 0), (0, V7X_LANES - n_gate)))
            gp, gs = matmul(hp, hs, (w_g,), prefix=(0,), n_out=V7X_LANES, epi="bias", out_dtype=F32,
                            col_extras=(b_g,), tn=V7X_LANES, w_is_nk=True)

            zeros = lambda *shape: jnp.zeros(shape, F32)
            yp, c, n, m = mlstm(qp, kp, vp, op, gp, ml_h_norm[jl], zeros(bp, ml_heads, ml_dv, ml_dk),
                                zeros(bp, ml_heads, ml_dk), zeros(bp, ml_heads),
                                batch=bp, t_len=t_len, y_dtype=BF16)
            c_p.append(c); n_p.append(n.reshape(bp, ml_heads, ml_dk)); m_p.append(m.reshape(bp, ml_heads))
            f32 = lambda a_: a_.astype(F32)
            ys, c, n, m = mlstm(f32(qs), f32(ks), f32(vs), os_, gs, ml_h_norm[jl], state_C[jl], state_n[jl],
                                state_m[jl], batch=bs, t_len=ts, y_dtype=F32)
            c_s.append(c); n_s.append(n.reshape(bs, ml_heads, ml_dk)); m_s.append(m.reshape(bs, ml_heads))
            xp, xs = matmul(yp, ys.astype(BF16), (ml_w_out,), prefix=(jl,), n_out=d, epi="residual",
                            out_dtype=F32, row_extras=((xp, xs),))
        xp, xs = _ffn_half(xp, xs, norm_g[l, 2], w_ffn_gate, w_ffn_up, w_down, (l, 1))

    stack = lambda rows: rows[0][None] if len(rows) == 1 else jnp.stack(rows)
    return (xp.reshape(bp, t_len, d), xs.reshape(bs, ts, d),
            stack(kp_rows), stack(vp_rows), stack(c_p), stack(n_p), stack(m_p),
            stack(ks_rows), stack(vs_rows), stack(c_s), stack(n_s), stack(m_s))
```

```python
import functools
import math

import jax
import jax.numpy as jnp
from jax import lax
from jax.experimental import pallas as pl
from jax.experimental.pallas import tpu as pltpu

F32 = jnp.float32
BF16 = jnp.bfloat16

NORM_EPS = 1e-6
ML_CHUNK = 128
DEPTH_MIXERS = 2

V7X_VMEM_BYTES = 64 * 1024 * 1024
V7X_LANES = 128
V7X_MXU_DIM = 256
VMEM_LIMIT_BYTES = V7X_VMEM_BYTES - 6 * 1024 * 1024

LOG2E = math.log2(math.e)
MASK_NEG = -1e30
SB_KEY_BLOCK = 128
SB_ROW_CHUNK = 256


def _params(n_axes):
    return pltpu.CompilerParams(dimension_semantics=("arbitrary",) * n_axes,
                                vmem_limit_bytes=VMEM_LIMIT_BYTES)


def _largest_tile(m, cap):
    t = min(m, cap)
    while m % t:
        t -= 8
    return t


def _log_sigmoid(x):
    return jnp.minimum(x, 0.0) - jnp.log(1.0 + jnp.exp(-jnp.abs(x)))


def _log2_sigmoid(x2):
    neg_abs = lax.bitcast_convert_type(lax.bitcast_convert_type(x2, jnp.uint32) | jnp.uint32(1 << 31), F32)
    return jnp.minimum(x2, 0.0) - jnp.log2(1.0 + jnp.exp2(neg_abs))


def _sigmoid(x):
    return 1.0 / (1.0 + jnp.exp(-x))


def _split_bf16(x):
    hi = x.astype(BF16)
    lo = (x - hi.astype(F32)).astype(BF16)
    return hi, lo


def _rmsnorm_body(x_ref, g_ref, o_ref):
    x = x_ref[...]
    ms = jnp.mean(x * x, axis=-1, keepdims=True)
    o_ref[...] = (x * lax.rsqrt(ms + NORM_EPS) * g_ref[...]).astype(o_ref.dtype)


def rmsnorm(x, g):
    m, d = x.shape
    tm = _largest_tile(m, 512)
    return pl.pallas_call(
        _rmsnorm_body,
        grid=(m // tm,),
        in_specs=[pl.BlockSpec((tm, d), lambda i: (i, 0)),
                  pl.BlockSpec((1, d), lambda i: (0, 0))],
        out_specs=pl.BlockSpec((tm, d), lambda i: (i, 0)),
        out_shape=jax.ShapeDtypeStruct((m, d), BF16),
        compiler_params=_params(1),
        name="rmsnorm",
    )(x, g.reshape(1, d))


def _epilogue(accs, extras, *, epi, coef, out_dtype):
    if epi == "swiglu":
        g, u = accs
        return (g * _sigmoid(g) * u).astype(out_dtype)
    (acc,) = accs
    if epi == "cast":
        return (acc * coef).astype(out_dtype) if coef != 1.0 else acc.astype(out_dtype)
    if epi == "bias":
        (b,) = extras
        return (acc + b).astype(out_dtype)
    if epi == "headnorm":
        (gain,) = extras
        outs = []
        for h in range(acc.shape[1] // V7X_LANES):
            a = acc[:, h * V7X_LANES:(h + 1) * V7X_LANES]
            ms = jnp.mean(a * a, axis=-1, keepdims=True)
            y = a * lax.rsqrt(ms + NORM_EPS) * gain[:, h * V7X_LANES:(h + 1) * V7X_LANES]
            outs.append(y * coef if coef != 1.0 else y)
        return jnp.concatenate(outs, axis=1).astype(out_dtype)
    if epi == "residual":
        (x,) = extras
        return (x + coef * acc).astype(out_dtype)
    raise ValueError(epi)


def _mm_body(*refs, n_w, n_row_extra, n_col_extra, epi, coef, w_is_nk):
    ap_ref, as_ref = refs[0], refs[1]
    contract = (((1,), (1 if w_is_nk else 0,)), ((), ()))
    w_refs = refs[2:2 + n_w]
    pos = 2 + n_w
    xp_refs = refs[pos:pos + n_row_extra]
    xs_refs = refs[pos + n_row_extra:pos + 2 * n_row_extra]
    pos += 2 * n_row_extra
    c_refs = refs[pos:pos + n_col_extra]
    op_ref, os_ref = refs[pos + n_col_extra:]

    wb = [w[...].astype(BF16) for w in w_refs]
    col = [c[...] for c in c_refs]

    def run(a_ref, x_refs, o_ref):
        a = a_ref[...]
        accs = [lax.dot_general(a, w, contract, preferred_element_type=F32) for w in wb]
        extras = [x[...] for x in x_refs] + col
        o_ref[...] = _epilogue(accs, extras, epi=epi, coef=coef, out_dtype=o_ref.dtype)

    run(ap_ref, xp_refs, op_ref)

    @pl.when(pl.program_id(0) == 0)
    def _():
        run(as_ref, xs_refs, os_ref)


def _mm_tiles(mp, k, n_out, n_w, tn):
    if tn is not None:
        return _largest_tile(mp, 2048), tn, True
    if n_w == 1 and k <= 4096 and n_out % (2 * V7X_MXU_DIM) == 0:
        return _largest_tile(mp, 1024), 2 * V7X_MXU_DIM, False
    return _largest_tile(mp, 2048 if k <= 4096 else 1024), V7X_MXU_DIM, True


def matmul(ap, as_, ws, *, prefix, n_out, col_off=0, epi, coef=1.0, out_dtype,
           row_extras=(), col_extras=(), tn=None, w_is_nk=False):
    mp, k = ap.shape
    ms = as_.shape[0]
    tm, tn, single_a = _mm_tiles(mp, k, n_out, len(ws), tn)
    ni, nj = mp // tm, n_out // tn
    assert n_out % tn == 0 and mp % tm == 0 and col_off % tn == 0
    npre = len(prefix)
    w_block = (tn, k) if w_is_nk else (k, tn)

    def w_map(i, j):
        jw = j + col_off // tn
        return (*prefix, jw, 0) if w_is_nk else (*prefix, 0, jw)

    def s_col(i, j):
        return (0, jnp.where(i == 0, j, nj - 1))

    in_specs = [pl.BlockSpec((tm, k), lambda i, j: (i, 0), pipeline_mode=pl.Buffered(1 if single_a else 2)),
                pl.BlockSpec((ms, k), lambda i, j: (0, 0))]
    in_specs += [pl.BlockSpec((None,) * npre + w_block, w_map) for _ in ws]
    in_specs += [pl.BlockSpec((tm, tn), lambda i, j: (i, j)) for _ in row_extras]
    in_specs += [pl.BlockSpec((ms, tn), s_col) for _ in row_extras]
    in_specs += [pl.BlockSpec((1, tn), lambda i, j: (0, j)) for _ in col_extras]
    args = [ap, as_, *ws, *[e[0] for e in row_extras], *[e[1] for e in row_extras], *col_extras]

    body = functools.partial(_mm_body, n_w=len(ws), n_row_extra=len(row_extras),
                             n_col_extra=len(col_extras), epi=epi, coef=coef, w_is_nk=w_is_nk)
    return pl.pallas_call(
        body,
        grid=(ni, nj),
        in_specs=in_specs,
        out_specs=[pl.BlockSpec((tm, tn), lambda i, j: (i, j)),
                   pl.BlockSpec((ms, tn), s_col)],
        out_shape=[jax.ShapeDtypeStruct((mp, n_out), out_dtype),
                   jax.ShapeDtypeStruct((ms, n_out), out_dtype)],
        compiler_params=_params(2),
        name="mm_" + epi,
    )(*args)


SB_UNROLL = 8
SB_STAGE_LAGS = ((0, 0), (0, 1), (1, 2), (3, 1), (4, 0), (4, 1))


def _sb_schedule(t_len):
    kb_sz, rc = SB_KEY_BLOCK, SB_ROW_CHUNK
    r_per = rc // kb_sz
    items = [(kb, c) for kb in reversed(range(t_len // kb_sz)) for c in range(kb // r_per, t_len // rc)]
    n_trip = -(-(len(items) + max(lag for lag, _ in SB_STAGE_LAGS)) // SB_UNROLL) * SB_UNROLL
    spare = (0, t_len, r_per)

    def entry(i):
        if not 0 <= i < len(items):
            return spare
        kb, c = items[i]
        return kb, c * rc, (kb - c * r_per) if c == kb // r_per else r_per

    cols = [[entry(t - lag)[f] for t in range(n_trip)] for lag, f in SB_STAGE_LAGS]
    cols[1] = [min(r, t_len - rc) for r in cols[1]]
    return n_trip, jnp.asarray(cols, jnp.int32)


def _sb_prompt_body(tbl_ref, q_ref, k_ref, v_ref, bias_ref, o_ref,
                    ktd, vbd, acc, carry, bvar, *rings, t_len, n_trip):
    z_raw, s_hl, cs_raw, s_w = (rings[2 * i:2 * i + 2] for i in range(4))
    s_ls = rings[8:8 + SB_UNROLL]
    kb_sz, rc = SB_KEY_BLOCK, SB_ROW_CHUNK
    nkb, r_per = t_len // kb_sz, rc // kb_sz
    hd = V7X_LANES

    @pl.when((pl.program_id(0) == 0) & (pl.program_id(1) == 0))
    def _():
        ktd[...] = jnp.zeros_like(ktd)
        vbd[...] = jnp.zeros_like(vbd)

    for kb in range(nkb):
        rows = slice(kb * kb_sz, (kb + 1) * kb_sz)
        for e in range(2):
            cols = slice(e * hd, (e + 1) * hd)
            ktd[kb, e * hd:(e + 1) * hd, e * kb_sz:(e + 1) * kb_sz] = k_ref[rows, cols].T.astype(BF16)
            vbd[kb, e * kb_sz:(e + 1) * kb_sz, cols] = v_ref[rows, cols].astype(BF16)

    for buf in (acc, carry, *rings):
        buf[...] = jnp.zeros_like(buf)

    bias = bias_ref[...]
    ri = lax.broadcasted_iota(jnp.int32, (rc, 2 * kb_sz), 0)
    si = lax.broadcasted_iota(jnp.int32, (rc, 2 * kb_sz), 1) & (kb_sz - 1)
    for d in range(r_per):
        bvar[d] = jnp.where(d * kb_sz + si < ri, bias, MASK_NEG)
    bvar[r_per] = jnp.broadcast_to(bias, (rc, 2 * kb_sz))

    jr = lax.broadcasted_iota(jnp.int32, (4 * kb_sz, 2 * kb_sz), 0) & (2 * kb_sz - 1)
    jc = lax.broadcasted_iota(jnp.int32, (4 * kb_sz, 2 * kb_sz), 1)
    same_head = (jr // kb_sz) == (jc // kb_sz)
    tri = jnp.where(same_head & ((jr > jc) | ((jc & (kb_sz - 1)) == 0)), 1.0, 0.0).astype(BF16)

    def trip(t, phase):
        cur, prev = phase % 2, 1 - phase % 2
        rows = lambda r: pl.ds(pl.multiple_of(tbl_ref[r, t], rc), rc)
        z_raw[cur][...] = jnp.dot(q_ref[rows(1), :], ktd[tbl_ref[0, t]], preferred_element_type=F32)
        z = z_raw[prev][...] + bvar[tbl_ref[2, t]]
        ls = _log2_sigmoid(z)
        hi, lo = _split_bf16(ls - z)
        s_ls[phase][...] = ls
        for e in range(2):
            s_ls[phase][:, e * kb_sz:e * kb_sz + 1] = z[:, e * kb_sz:e * kb_sz + 1]
        s_hl[cur][...] = jnp.concatenate([hi, lo], axis=1)
        cs_raw[cur][...] = jnp.dot(s_hl[prev][...], tri, preferred_element_type=F32)
        rows_e2 = rows(3)
        cs = cs_raw[prev][...]
        s_w[cur][...] = jnp.exp2(s_ls[(phase - 2) % SB_UNROLL][...] + cs + carry[rows_e2, :]).astype(BF16)
        carry[rows_e2, :] += jnp.concatenate(
            [jnp.broadcast_to(cs[:, e * kb_sz:e * kb_sz + 1], (rc, kb_sz)) for e in range(2)], axis=1)
        acc[rows(5), :] += jnp.dot(s_w[prev][...], vbd[tbl_ref[4, t]], preferred_element_type=F32)

    def trips(u, _):
        for phase in range(SB_UNROLL):
            trip(SB_UNROLL * u + phase, phase)
        return 0

    lax.fori_loop(0, n_trip // SB_UNROLL, trips, 0)
    o_ref[...] = acc[0:t_len, :].astype(o_ref.dtype)


def sb_attend_prompt(q, k, v, bias, *, batch, t_len):
    m, d = q.shape
    w2 = 2 * V7X_LANES
    n_pair = d // w2
    nkb = t_len // SB_KEY_BLOCK
    rc = SB_ROW_CHUNK
    assert t_len % rc == 0
    n_trip, tables = _sb_schedule(t_len)
    spec = pl.BlockSpec((t_len, w2), lambda b, p, tbl: (b, p))
    grid_spec = pltpu.PrefetchScalarGridSpec(
        num_scalar_prefetch=1,
        grid=(batch, n_pair),
        in_specs=[spec, spec, spec, pl.BlockSpec((1, w2), lambda b, p, tbl: (0, p))],
        out_specs=spec,
        scratch_shapes=[pltpu.VMEM((nkb, w2, w2), BF16),
                        pltpu.VMEM((nkb, w2, w2), BF16),
                        pltpu.VMEM((t_len + rc, w2), F32),
                        pltpu.VMEM((t_len + rc, w2), F32),
                        pltpu.VMEM((rc // SB_KEY_BLOCK + 1, rc, w2), F32),
                        *[pltpu.VMEM((rc, w2), F32)] * 2,
                        *[pltpu.VMEM((rc, 2 * w2), BF16)] * 2,
                        *[pltpu.VMEM((rc, w2), F32)] * 2,
                        *[pltpu.VMEM((rc, w2), BF16)] * 2,
                        *[pltpu.VMEM((rc, w2), F32)] * SB_UNROLL],
    )
    return pl.pallas_call(
        functools.partial(_sb_prompt_body, t_len=t_len, n_trip=n_trip),
        grid_spec=grid_spec,
        out_shape=jax.ShapeDtypeStruct((m, d), BF16),
        compiler_params=_params(2),
        name="sb_attend_prompt",
    )(tables, q, k, v, jnp.repeat(bias * LOG2E, V7X_LANES).reshape(1, d))


SB_HEAD_PITCH = SB_KEY_BLOCK + 4


def _sb_sample_body(pt_ref, q_ref, knew_ref, vnew_ref, *refs, n_steps, per_step, ts):
    kpg_refs, vpg_refs = refs[:per_step], refs[per_step:2 * per_step]
    brow_ref, o_ref, qbd, acc, carry = refs[2 * per_step:2 * per_step + 5]
    hm_bufs = refs[2 * per_step + 5:]
    j = pl.program_id(1)
    hd, pg, pitch = V7X_LANES, SB_KEY_BLOCK, SB_HEAD_PITCH
    n_tile = qbd.shape[0]
    rows_per = 2 * ts

    jr = lax.broadcasted_iota(jnp.int32, (2 * pg, 2 * pg), 0) & (pg - 1)
    jc = lax.broadcasted_iota(jnp.int32, (2 * pg, 2 * pg), 1)
    tri_ones = jnp.where((jc >= pg) | (jr > jc), 1.0, 0.0).astype(BF16)

    def regroup(src_ref, dst):
        for p in range(pg):
            for g in range(src_ref.shape[1] // 8):
                dst[pl.ds(8 * g * pitch + p, 8, stride=pitch), :] = src_ref[p, 8 * g:8 * g + 8, :]

    def head_pair(buf, n):
        return jnp.concatenate([buf[h * pitch:h * pitch + pg, :].astype(BF16) for h in (2 * n, 2 * n + 1)],
                               axis=1)

    def process(k_ref, v_ref, khm, vhm, masked):
        regroup(k_ref, khm)
        regroup(v_ref, vhm)
        zs = []
        for n in range(n_tile):
            zs.append(lax.dot_general(qbd[n].astype(BF16), head_pair(khm, n), (((1,), (1,)), ((), ())),
                                      preferred_element_type=F32))
        z = jnp.concatenate(zs, axis=0) + brow_ref[...]
        if masked:
            qi = lax.broadcasted_iota(jnp.int32, z.shape, 0) % ts
            si = lax.broadcasted_iota(jnp.int32, z.shape, 1)
            z = jnp.where(si < qi, z, MASK_NEG)
        ls = _log2_sigmoid(z)
        hi, lo = _split_bf16(ls - z)
        cs = jnp.dot(jnp.concatenate([hi, lo], axis=1), tri_ones,
                     preferred_element_type=F32)
        w = jnp.exp2(ls + cs[:, :pg] + carry[...]).astype(BF16)
        for n in range(n_tile):
            acc[n] += jnp.dot(w[n * rows_per:(n + 1) * rows_per, :], head_pair(vhm, n),
                              preferred_element_type=F32)
        carry[...] += cs[:, pg:]

    @pl.when(j == 0)
    def _():
        qbd[...] = jnp.zeros_like(qbd)
        for n in range(n_tile):
            for e in range(2):
                qbd[n, e * ts:(e + 1) * ts, e * hd:(e + 1) * hd] = \
                    q_ref[:, (2 * n + e) * hd:(2 * n + e + 1) * hd]
        acc[...] = jnp.zeros_like(acc)
        carry[...] = jnp.zeros_like(carry)
        process(knew_ref, vnew_ref, hm_bufs[0], hm_bufs[1], True)

    n_hm = len(hm_bufs) // 2
    for i, (k_ref, v_ref) in enumerate(zip(kpg_refs, vpg_refs)):
        process(k_ref, v_ref, hm_bufs[2 * (i % n_hm)], hm_bufs[2 * (i % n_hm) + 1], False)

    @pl.when(j == n_steps - 1)
    def _():
        for n in range(n_tile):
            for e in range(2):
                o_ref[:, (2 * n + e) * hd:(2 * n + e + 1) * hd] = \
                    acc[n, e * ts:(e + 1) * ts, e * hd:(e + 1) * hd]


def sb_attend_sample(q, k_new, v_new, cache_k, cache_v, page_table, bias, *, layer, ts):
    m, d = q.shape
    bs = m // ts
    n_pages = page_table.shape[1]
    n_layers, n_phys, pg = cache_k.shape[:3]
    assert pg == SB_KEY_BLOCK and ts % 8 == 0
    heads, hd = cache_k.shape[3:]
    assert hd == V7X_LANES and heads * hd == d and heads % 8 == 0
    n_tile = heads // 2
    rows = heads * ts
    pad = lambda a: jnp.pad(a.reshape(bs, ts, heads, hd), ((0, 0), (0, pg - ts), (0, 0), (0, 0)))
    brow = jnp.broadcast_to(jnp.repeat(bias * LOG2E, ts)[:, None], (rows, pg))

    per_step = max(p for p in (4, 2, 1) if n_pages % p == 0)
    n_steps = n_pages // per_step
    page_spec = lambda i: pl.BlockSpec(
        (None, None, pg, heads, hd), lambda b, j, pt: (layer, pt[b, n_pages - 1 - (j * per_step + i)], 0, 0, 0))
    new_map = lambda b, j, pt: (b, 0, 0, 0)
    grid_spec = pltpu.PrefetchScalarGridSpec(
        num_scalar_prefetch=1,
        grid=(bs, n_steps),
        in_specs=[pl.BlockSpec((ts, d), lambda b, j, pt: (b, 0)),
                  pl.BlockSpec((None, pg, heads, hd), new_map),
                  pl.BlockSpec((None, pg, heads, hd), new_map),
                  *[page_spec(i) for i in range(per_step)],
                  *[page_spec(i) for i in range(per_step)],
                  pl.BlockSpec((rows, pg), lambda b, j, pt: (0, 0))],
        out_specs=pl.BlockSpec((ts, d), lambda b, j, pt: (b, 0)),
        scratch_shapes=[pltpu.VMEM((n_tile, 2 * ts, 2 * hd), F32),
                        pltpu.VMEM((n_tile, 2 * ts, 2 * hd), F32),
                        pltpu.VMEM((rows, pg), F32),
                        *[pltpu.VMEM((heads * SB_HEAD_PITCH, hd), F32)] * (2 * min(per_step, 2))],
    )
    return pl.pallas_call(
        functools.partial(_sb_sample_body, n_steps=n_steps, per_step=per_step, ts=ts),
        grid_spec=grid_spec,
        out_shape=jax.ShapeDtypeStruct((m, d), F32),
        compiler_params=_params(2),
        name="sb_attend_sample",
    )(page_table, q, pad(k_new), pad(v_new), *[cache_k] * per_step, *[cache_v] * per_step, brow)


def _mxu_operand(x):
    y = x.astype(BF16)
    return y if x.shape[0] >= 16 else y.astype(F32)


def _mlstm_body(q_ref, k_ref, v_ref, o_ref, ir_ref, ic_ref, fr_ref, fc_ref, hg_ref,
                c0_ref, n0_ref, m0_ref, y_ref, c_ref, n_ref, m_ref, ct, ns, ms, *, chunk, n_chunks):
    c = pl.program_id(1)
    ln = chunk
    heads, dk, dv = ct.shape

    @pl.when(c == 0)
    def _():
        for h in range(heads):
            ct[h] = c0_ref[h].T
        ns[...] = n0_ref[...]
        ms[...] = m0_ref[...]

    ti = lax.broadcasted_iota(jnp.int32, (ln, ln), 0)
    si = lax.broadcasted_iota(jnp.int32, (ln, ln), 1)
    causal = si <= ti

    for h in range(heads):
        qk_cols = slice(h * dk, (h + 1) * dk)
        v_cols = slice(h * dv, (h + 1) * dv)
        q = _mxu_operand(q_ref[:, qk_cols])
        k = _mxu_operand(k_ref[:, qk_cols])
        v = _mxu_operand(v_ref[:, v_cols])
        lf_r = _log_sigmoid(fr_ref[h])
        lf_c = _log_sigmoid(fc_ref[h])
        b_col = jnp.sum(jnp.where(causal, lf_r, 0.0), axis=1, keepdims=True)
        b_row = jnp.sum(jnp.where(ti <= si, lf_c, 0.0), axis=0, keepdims=True)
        dmat = jnp.where(causal, b_col - b_row + ir_ref[h], -jnp.inf)
        m_prev = ms[h]
        g = b_col + m_prev
        m_t = jnp.maximum(g, jnp.max(dmat, axis=1, keepdims=True))
        s = lax.dot_general(q, k, (((1,), (1,)), ((), ())), preferred_element_type=F32)
        w = jnp.exp(dmat - m_t) * s
        w_carry = jnp.exp(g - m_t)
        num = (w_carry * jnp.dot(q, _mxu_operand(ct[h]), preferred_element_type=F32)
               + jnp.dot(_mxu_operand(w), v, preferred_element_type=F32))
        qn = jnp.sum(q.astype(F32) * ns[h], axis=1, keepdims=True)
        den = w_carry * qn + jnp.sum(w, axis=1, keepdims=True)
        hh = num / jnp.maximum(jnp.abs(den), jnp.exp(-m_t))

        hms = jnp.mean(hh * hh, axis=1, keepdims=True)
        hn = hh * lax.rsqrt(hms + NORM_EPS) * hg_ref[:, v_cols]
        y_ref[:, v_cols] = (_sigmoid(o_ref[:, v_cols]) * hn).astype(y_ref.dtype)

        m_new = m_t[ln - 1:ln, :]
        b_last = b_col[ln - 1:ln, :]
        decay = jnp.exp(b_last - b_col + ic_ref[h] - m_new)
        c_scale = jnp.exp(b_last + m_prev - m_new)
        kd = decay * k.astype(F32)
        ct[h] = c_scale * ct[h] + lax.dot_general(
            _mxu_operand(kd), v, (((0,), (0,)), ((), ())), preferred_element_type=F32)
        ns[h] = c_scale * ns[h] + jnp.sum(kd, axis=0, keepdims=True)
        ms[h] = m_new

    @pl.when(c == n_chunks - 1)
    def _():
        for h in range(heads):
            c_ref[h] = ct[h].T
        n_ref[...] = ns[...]
        m_ref[...] = ms[...]


def mlstm(q, k, v, o, gates, h_gain, c0, n0, m0, *, batch, t_len, y_dtype):
    heads, dv, dk = c0.shape[1:]
    chunk = ML_CHUNK if t_len % ML_CHUNK == 0 else t_len
    nc = t_len // chunk
    gt = gates[:, :2 * heads].reshape(batch, nc, chunk, 2, heads)
    gt = jnp.transpose(gt, (3, 0, 4, 1, 2))
    i_row, f_row = gt[0][..., None, :], gt[1][..., None, :]
    i_col, f_col = gt[0][..., :, None], gt[1][..., :, None]

    tok = lambda width: pl.BlockSpec((chunk, heads * width), lambda b, c: (b * nc + c, 0))
    row = pl.BlockSpec((None, heads, None, 1, chunk), lambda b, c: (b, 0, c, 0, 0))
    col = pl.BlockSpec((None, heads, None, chunk, 1), lambda b, c: (b, 0, c, 0, 0))
    st = lambda r, w: pl.BlockSpec((None, heads, r, w), lambda b, c: (b, 0, 0, 0))
    return pl.pallas_call(
        functools.partial(_mlstm_body, chunk=chunk, n_chunks=nc),
        grid=(batch, nc),
        in_specs=[tok(dk), tok(dk), tok(dv), tok(dv), row, col, row, col,
                  pl.BlockSpec((1, heads * dv), lambda b, c: (0, 0)),
                  st(dv, dk), st(1, dk), st(1, 1)],
        out_specs=[tok(dv), st(dv, dk), st(1, dk), st(1, 1)],
        out_shape=[jax.ShapeDtypeStruct((batch * t_len, heads * dv), y_dtype),
                   jax.ShapeDtypeStruct((batch, heads, dv, dk), F32),
                   jax.ShapeDtypeStruct((batch, heads, 1, dk), F32),
                   jax.ShapeDtypeStruct((batch, heads, 1, 1), F32)],
        scratch_shapes=[pltpu.VMEM((heads, dk, dv), F32), pltpu.VMEM((heads, 1, dk), F32),
                        pltpu.VMEM((heads, 1, 1), F32)],
        compiler_params=_params(2),
        name="mlstm",
    )(q, k, v, o, i_row, i_col, f_row, f_col, h_gain.reshape(1, heads * dv),
      c0, n0.reshape(batch, heads, 1, dk), m0.reshape(batch, heads, 1, 1))


def _ffn_half(xp, xs, gain, w_gate, w_up, w_down, prefix):
    hp, hs = rmsnorm(xp, gain), rmsnorm(xs, gain)
    d_ff = w_gate.shape[-1]
    ap, as_ = matmul(hp, hs, (w_gate, w_up), prefix=prefix, n_out=d_ff, epi="swiglu", out_dtype=BF16)
    return matmul(ap, as_, (w_down,), prefix=prefix, n_out=xp.shape[1], epi="residual", coef=0.5,
                  out_dtype=F32, row_extras=((xp, xs),))


def kernel(x_prompt, x_sample, cache_k, cache_v, state_C, state_n, state_m, page_table, norm_g, w_ffn_gate, w_ffn_up, w_ffn_down, sb_w_in, sb_q_norm, sb_k_norm, sb_logit_bias, sb_w_out, ml_w_in, ml_b_gates, ml_h_norm, ml_w_out):
    bp, t_len, d = x_prompt.shape
    bs, ts, _ = x_sample.shape
    depth = norm_g.shape[0]
    sb_heads = sb_logit_bias.shape[1]
    sb_hd = d // sb_heads
    assert sb_hd == V7X_LANES
    ml_heads, ml_dv, ml_dk = state_C.shape[2:]

    xp = x_prompt.reshape(bp * t_len, d)
    xs = x_sample.reshape(bs * ts, d)
    w_down = w_ffn_down
    kp_rows, vp_rows, ks_rows, vs_rows = [], [], [], []
    c_p, n_p, m_p, c_s, n_s, m_s = [], [], [], [], [], []

    for l in range(depth):
        xp, xs = _ffn_half(xp, xs, norm_g[l, 0], w_ffn_gate, w_ffn_up, w_down, (l, 0))
        hp, hs = rmsnorm(xp, norm_g[l, 1]), rmsnorm(xs, norm_g[l, 1])
        if l % DEPTH_MIXERS == 0:
            a = l // DEPTH_MIXERS
            tile = lambda g: jnp.tile(g, d // sb_hd).reshape(1, d)
            qp, qs = matmul(hp, hs, (sb_w_in,), prefix=(a,), n_out=d, col_off=0, epi="headnorm",
                            coef=LOG2E / math.sqrt(sb_hd), out_dtype=BF16, col_extras=(tile(sb_q_norm[a]),))
            kp, ks = matmul(hp, hs, (sb_w_in,), prefix=(a,), n_out=d, col_off=d, epi="headnorm",
                            out_dtype=F32, col_extras=(tile(sb_k_norm[a]),))
            vp, vs = matmul(hp, hs, (sb_w_in,), prefix=(a,), n_out=d, col_off=2 * d, epi="cast",
                            out_dtype=F32)
            att_p = sb_attend_prompt(qp, kp, vp, sb_logit_bias[a], batch=bp, t_len=t_len)
            att_s = sb_attend_sample(qs.astype(F32), ks, vs, cache_k, cache_v, page_table,
                                     sb_logit_bias[a], layer=a, ts=ts)
            xp, xs = matmul(att_p, att_s.astype(BF16), (sb_w_out,), prefix=(a,), n_out=d, epi="residual",
                            out_dtype=F32, row_extras=((xp, xs),))
            kp_rows.append(kp.reshape(bp, t_len, sb_heads, sb_hd))
            vp_rows.append(vp.reshape(bp, t_len, sb_heads, sb_hd))
            ks_rows.append(ks.reshape(bs, ts, sb_heads, sb_hd))
            vs_rows.append(vs.reshape(bs, ts, sb_heads, sb_hd))
        else:
            jl = l // DEPTH_MIXERS
            s_qk = ml_heads * ml_dk
            w_in_nk = jnp.swapaxes(ml_w_in, 1, 2)
            proj = functools.partial(matmul, hp, hs, (w_in_nk,), prefix=(jl,), w_is_nk=True)
            qp, qs = proj(n_out=s_qk, col_off=0, epi="cast", out_dtype=BF16)
            kp, ks = proj(n_out=s_qk, col_off=s_qk, epi="cast", coef=ml_dk ** -0.5, out_dtype=BF16)
            vp, vs = proj(n_out=d, col_off=2 * s_qk, epi="cast", out_dtype=BF16)
            op, os_ = proj(n_out=d, col_off=2 * s_qk + d, epi="cast", out_dtype=F32)
            n_gate = 2 * ml_heads
            w_g = jnp.pad(w_in_nk[jl:jl + 1, 2 * s_qk + 2 * d:, :], ((0, 0), (0, V7X_LANES - n_gate), (0, 0)))
            b_g = jnp.pad(ml_b_gates[jl].reshape(1, n_gate), ((0, 0), (0, V7X_LANES - n_gate)))
            gp, gs = matmul(hp, hs, (w_g,), prefix=(0,), n_out=V7X_LANES, epi="bias", out_dtype=F32,
                            col_extras=(b_g,), tn=V7X_LANES, w_is_nk=True)

            zeros = lambda *shape: jnp.zeros(shape, F32)
            yp, c, n, m = mlstm(qp, kp, vp, op, gp, ml_h_norm[jl], zeros(bp, ml_heads, ml_dv, ml_dk),
                                zeros(bp, ml_heads, ml_dk), zeros(bp, ml_heads),
                                batch=bp, t_len=t_len, y_dtype=BF16)
            c_p.append(c); n_p.append(n.reshape(bp, ml_heads, ml_dk)); m_p.append(m.reshape(bp, ml_heads))
            f32 = lambda a_: a_.astype(F32)
            ys, c, n, m = mlstm(f32(qs), f32(ks), f32(vs), os_, gs, ml_h_norm[jl], state_C[jl], state_n[jl],
                                state_m[jl], batch=bs, t_len=ts, y_dtype=F32)
            c_s.append(c); n_s.append(n.reshape(bs, ml_heads, ml_dk)); m_s.append(m.reshape(bs, ml_heads))
            xp, xs = matmul(yp, ys.astype(BF16), (ml_w_out,), prefix=(jl,), n_out=d, epi="residual",
                            out_dtype=F32, row_extras=((xp, xs),))
        xp, xs = _ffn_half(xp, xs, norm_g[l, 2], w_ffn_gate, w_ffn_up, w_down, (l, 1))

    stack = lambda rows: rows[0][None] if len(rows) == 1 else jnp.stack(rows)
    return (xp.reshape(bp, t_len, d), xs.reshape(bs, ts, d),
            stack(kp_rows), stack(vp_rows), stack(c_p), stack(n_p), stack(m_p),
            stack(ks_rows), stack(vs_rows), stack(c_s), stack(n_s), stack(m_s))
```
